```python
import jax, jax.numpy as jnp
from jax import lax
import numpy as np

D_MODEL = 2048
BATCH = 8
SEQ = 2048
DEPTH = 1

CHUNK = 64
Q_BLOCK = 128
D_PLE = 256
EPS = 1e-6

POOL_WIDTH = D_MODEL // 2
POOL_WINDOWS = (2, 4, 8, 16)
POOL_GROUPS = len(POOL_WINDOWS)
POOL_GROUP_WIDTH = POOL_WIDTH // POOL_GROUPS

V_HEAD = 128
MLA_HEADS = (D_MODEL // 2) // V_HEAD
MLA_WIDTH = MLA_HEADS * V_HEAD
QK_NOPE = 128
QK_ROPE = 64
QK_HEAD = QK_NOPE + QK_ROPE
Q_LORA = 512
KV_LORA = 512
ROPE_THETA = 10000.0

MIX_WIDTH = POOL_WIDTH + MLA_WIDTH
IN_COLS = POOL_WIDTH + Q_LORA + KV_LORA + QK_ROPE

PEER_HEADS = 8
PEER_NKEYS = 128
PEER_EXPERTS = PEER_NKEYS * PEER_NKEYS
PEER_DKEY = 256
PEER_HALF = PEER_DKEY // 2
PEER_TOPK = 16
PEER_TOK_BLOCK = 128

kernel_name = "hybrid_pool_mla_peer_ple"


def rms_norm(x, g):
    xf = x.astype(jnp.float32)
    y = xf * lax.rsqrt(jnp.mean(xf * xf, axis=-1, keepdims=True) + EPS)
    return (y * g.astype(jnp.float32)).astype(x.dtype)


def apply_rope(x, cos, sin):
    half = QK_ROPE // 2
    xf = x.astype(jnp.float32)
    x1, x2 = xf[..., :half], xf[..., half:]
    return jnp.concatenate([x1 * cos - x2 * sin, x2 * cos + x1 * sin], axis=-1).astype(x.dtype)


def pool_mixer(xp, w_pool, pool_scale):
    B, S, _ = xp.shape
    xf = xp.reshape(B, S, POOL_GROUPS, POOL_GROUP_WIDTH).astype(jnp.float32)
    cs = jnp.cumsum(xf, axis=1)
    t = jnp.arange(S)
    outs = []
    for g, w in enumerate(POOL_WINDOWS):
        csg = cs[:, :, g]
        lower = jnp.pad(csg, ((0, 0), (w, 0), (0, 0)))[:, :S]
        cnt = jnp.minimum(t + 1, w).astype(jnp.float32)[None, :, None]
        outs.append((csg - lower) / cnt - xf[:, :, g])
    d = jnp.stack(outs, axis=2).astype(xp.dtype)
    y = jnp.einsum('bsgc,gcd->bsgd', d, w_pool)
    return y.reshape(B, S, POOL_WIDTH) * pool_scale


def mla(c_q, c_kv, k_rope, positions, q_lat_gain, kv_lat_gain, w_uq, w_ukv, q_norm_gain, k_norm_gain):
    B, S, _ = c_q.shape
    q = (rms_norm(c_q, q_lat_gain) @ w_uq).reshape(B, S, MLA_HEADS, QK_HEAD)
    kv = (rms_norm(c_kv, kv_lat_gain) @ w_ukv).reshape(B, S, MLA_HEADS, QK_NOPE + V_HEAD)
    k_nope, v = kv[..., :QK_NOPE], kv[..., QK_NOPE:]
    k = jnp.concatenate([k_nope, jnp.broadcast_to(k_rope[:, :, None, :], (B, S, MLA_HEADS, QK_ROPE))], axis=-1)
    q = rms_norm(q, q_norm_gain)
    k = rms_norm(k, k_norm_gain)
    inv_freq = ROPE_THETA ** (-jnp.arange(0, QK_ROPE, 2, dtype=jnp.float32) / QK_ROPE)
    ang = positions.astype(jnp.float32)[:, :, None] * inv_freq
    cos, sin = jnp.cos(ang)[:, :, None, :], jnp.sin(ang)[:, :, None, :]
    q = jnp.concatenate([q[..., :QK_NOPE], apply_rope(q[..., QK_NOPE:], cos, sin)], axis=-1)
    k = jnp.concatenate([k[..., :QK_NOPE], apply_rope(k[..., QK_NOPE:], cos, sin)], axis=-1)

    nb = S // Q_BLOCK
    qb = q.reshape(B, nb, Q_BLOCK, MLA_HEADS, QK_HEAD).transpose(1, 0, 2, 3, 4)
    k_chunk = jnp.arange(S) // CHUNK
    scale = QK_HEAD ** -0.5

    def attend(args):
        q_blk, blk = args
        q_chunk = (blk * Q_BLOCK + jnp.arange(Q_BLOCK)) // CHUNK
        s = jnp.einsum('bqhd,bkhd->bhqk', q_blk, k, preferred_element_type=jnp.float32) * scale
        s = jnp.where(k_chunk[None, :] <= q_chunk[:, None], s, -jnp.inf)
        pr = jax.nn.softmax(s, axis=-1).astype(v.dtype)
        return jnp.einsum('bhqk,bkhd->bqhd', pr, v)

    o = lax.map(attend, (qb, jnp.arange(nb)))
    return o.transpose(1, 0, 2, 3, 4).reshape(B, S, MLA_WIDTH)


def peer(xn, w_pq, sub_k1, sub_k2, expert_u, expert_v):
    B, S, D = xn.shape
    q = (xn @ w_pq).reshape(B, S, PEER_HEADS, 2, PEER_HALF)
    s1 = jnp.einsum('bshd,hnd->bshn', q[..., 0, :], sub_k1, preferred_element_type=jnp.float32)
    s2 = jnp.einsum('bshd,hnd->bshn', q[..., 1, :], sub_k2, preferred_element_type=jnp.float32)
    v1, i1 = lax.top_k(s1, PEER_TOPK)
    v2, i2 = lax.top_k(s2, PEER_TOPK)
    cand = (v1[..., :, None] + v2[..., None, :]).reshape(B, S, PEER_HEADS, PEER_TOPK * PEER_TOPK)
    vs, ci = lax.top_k(cand, PEER_TOPK)
    e1 = jnp.take_along_axis(i1, ci // PEER_TOPK, axis=-1)
    e2 = jnp.take_along_axis(i2, ci % PEER_TOPK, axis=-1)
    eidx = e1 * PEER_NKEYS + e2
    gates = jax.nn.softmax(vs, axis=-1).astype(xn.dtype)

    T = B * S
    nblk = T // PEER_TOK_BLOCK
    kk = PEER_HEADS * PEER_TOPK
    xt = xn.reshape(nblk, PEER_TOK_BLOCK, D)
    it = eidx.reshape(nblk, PEER_TOK_BLOCK, kk)
    gt = gates.reshape(nblk, PEER_TOK_BLOCK, kk)

    def block(args):
        xb, ib, gb = args
        act = jax.nn.gelu(jnp.einsum('tkd,td->tk', expert_u[ib], xb))
        return jnp.einsum('tk,tkd->td', act * gb, expert_v[ib])

    return lax.map(block, (xt, it, gt)).reshape(B, S, D)


def setup_inputs(seed: int = 0) -> dict:
    key = jax.random.key(seed)
    ks = jax.random.split(key, 24)
    f32 = jnp.float32

    def nrm(k, shape, scale):
        return jax.random.normal(k, shape, f32) * scale

    def gain(k, n):
        return 1.0 + 0.02 * jax.random.normal(k, (DEPTH, n), f32)

    x = jax.random.normal(ks[0], (BATCH, SEQ, D_MODEL), f32)
    p = jax.random.normal(ks[1], (DEPTH, BATCH, SEQ, D_PLE), f32)
    offs = jax.random.randint(ks[2], (BATCH, 1), 0, 4096, dtype=jnp.int32)
    positions = offs + jnp.arange(SEQ, dtype=jnp.int32)[None, :]
    return {
        "x": x,
        "p": p,
        "positions": positions,
        "mix_norm_gain": gain(ks[3], D_MODEL),
        "w_in": nrm(ks[4], (DEPTH, D_MODEL, IN_COLS), D_MODEL ** -0.5),
        "w_pool": nrm(ks[5], (DEPTH, POOL_GROUPS, POOL_GROUP_WIDTH, POOL_GROUP_WIDTH), POOL_GROUP_WIDTH ** -0.5),
        "pool_scale": 1.0 + 0.1 * jax.random.normal(ks[6], (DEPTH, POOL_WIDTH), f32),
        "q_lat_gain": gain(ks[7], Q_LORA),
        "kv_lat_gain": gain(ks[8], KV_LORA),
        "w_uq": nrm(ks[9], (DEPTH, Q_LORA, MLA_HEADS * QK_HEAD), Q_LORA ** -0.5),
        "w_ukv": nrm(ks[10], (DEPTH, KV_LORA, MLA_HEADS * (QK_NOPE + V_HEAD)), KV_LORA ** -0.5),
        "q_norm_gain": gain(ks[11], QK_HEAD),
        "k_norm_gain": gain(ks[12], QK_HEAD),
        "w_out": nrm(ks[13], (DEPTH, MIX_WIDTH, D_MODEL), MIX_WIDTH ** -0.5),
        "ffn_norm_gain": gain(ks[14], D_MODEL),
        "w_pq": nrm(ks[15], (DEPTH, D_MODEL, PEER_HEADS * PEER_DKEY), D_MODEL ** -0.5),
        "sub_k1": nrm(ks[16], (DEPTH, PEER_HEADS, PEER_NKEYS, PEER_HALF), PEER_HALF ** -0.5),
        "sub_k2": nrm(ks[17], (DEPTH, PEER_HEADS, PEER_NKEYS, PEER_HALF), PEER_HALF ** -0.5),
        "expert_u": nrm(ks[18], (DEPTH, PEER_EXPERTS, D_MODEL), D_MODEL ** -0.5),
        "expert_v": nrm(ks[19], (DEPTH, PEER_EXPERTS, D_MODEL), PEER_HEADS ** -0.5),
        "ple_norm_gain": gain(ks[20], D_MODEL),
        "w_ple_gate": nrm(ks[21], (DEPTH, D_MODEL, D_MODEL), D_MODEL ** -0.5),
        "w_ple_proj": nrm(ks[22], (DEPTH, D_PLE, D_MODEL), D_PLE ** -0.5),
    }


def reference(x, p, positions, mix_norm_gain, w_in, w_pool, pool_scale, q_lat_gain, kv_lat_gain,
              w_uq, w_ukv, q_norm_gain, k_norm_gain, w_out, ffn_norm_gain, w_pq, sub_k1, sub_k2,
              expert_u, expert_v, ple_norm_gain, w_ple_gate, w_ple_proj):
    h = x
    c0 = POOL_WIDTH
    c1 = c0 + Q_LORA
    c2 = c1 + KV_LORA
    for i in range(DEPTH):
        z = rms_norm(h, mix_norm_gain[i]) @ w_in[i]
        y_pool = pool_mixer(z[..., :c0], w_pool[i], pool_scale[i])
        y_mla = mla(z[..., c0:c1], z[..., c1:c2], z[..., c2:], positions,
                    q_lat_gain[i], kv_lat_gain[i], w_uq[i], w_ukv[i], q_norm_gain[i], k_norm_gain[i])
        h = h + jnp.concatenate([y_pool, y_mla], axis=-1) @ w_out[i]
        h = h + peer(rms_norm(h, ffn_norm_gain[i]), w_pq[i], sub_k1[i], sub_k2[i], expert_u[i], expert_v[i])
        gate = jax.nn.sigmoid(rms_norm(h, ple_norm_gain[i]) @ w_ple_gate[i])
        h = h + gate * (p[i] @ w_ple_proj[i])
    return h
```

```python
import functools

import jax
import jax.numpy as jnp
from jax import lax
from jax.experimental import pallas as pl
from jax.experimental.pallas import tpu as pltpu

F32 = jnp.float32
BF16 = jnp.bfloat16

EPS = 1e-6
CHUNK = 64
POOL_WINDOWS = (2, 4, 8, 16)
POOL_HALO = 16
V_HEAD = 128
QK_NOPE = 128
QK_ROPE = 64
QK_HEAD = QK_NOPE + QK_ROPE
HEAD_SLOT = 256
ROPE_THETA = 10000.0
PEER_HEADS = 8
PEER_NKEYS = 128
PEER_HALF = 128
PEER_TOPK = 16
PEER_SLOTS = PEER_HEADS * PEER_TOPK

LANES = 128
VMEM_LIMIT_BYTES = 56 * 1024 * 1024

GATHER_RING = 4


def _rms(x):
    return x * lax.rsqrt(jnp.mean(x * x, axis=-1, keepdims=True) + EPS)


def _dot(a, b):
    return jnp.dot(a, b, preferred_element_type=F32)


def _dot_nt(a, b):
    return lax.dot_general(a, b, (((1,), (1,)), ((), ())), preferred_element_type=F32)


def _params(*sem):
    return pltpu.CompilerParams(dimension_semantics=sem, vmem_limit_bytes=VMEM_LIMIT_BYTES)


def _const_spec(shape):
    nd = len(shape)
    return pl.BlockSpec(shape, lambda *_: (0,) * nd)


def _mix_in_kernel(x_ref, pos_ref, g_ref, win_ref, gq_ref, gkv_ref, wuq_ref, wukv_ref, gqh_ref, gkh_ref,
                   invf_ref, zp_ref, q_ref, k_ref, v_ref, *, pool_w, q_lora, kv_lora, n_heads):
    xn = _rms(x_ref[...]) * g_ref[...]
    z = _dot(xn.astype(BF16), win_ref[...])
    zp_ref[...] = z[:, :pool_w]
    c1 = pool_w + q_lora
    c2 = c1 + kv_lora
    cq = _rms(z[:, pool_w:c1]) * gq_ref[...]
    ckv = _rms(z[:, c1:c2]) * gkv_ref[...]
    kr = z[:, c2:c2 + LANES]
    qx = _dot(cq.astype(BF16), wuq_ref[...])
    kvx = _dot(ckv.astype(BF16), wukv_ref[...])

    ang = pos_ref[...].astype(F32) * invf_ref[...]
    cos = jnp.cos(ang)
    sin = jnp.sin(ang)
    lane = lax.broadcasted_iota(jnp.int32, ang.shape, 1)
    first_half = lane < QK_ROPE // 2
    sin_signed = jnp.where(first_half, -sin, sin)

    def rope(y):
        partner = jnp.where(first_half, pltpu.roll(y, LANES - QK_ROPE // 2, 1), pltpu.roll(y, QK_ROPE // 2, 1))
        return y * cos + partner * sin_signed

    gq = gqh_ref[...]
    gk = gkh_ref[...]
    inv_w = 1.0 / QK_HEAD
    kr_ss = jnp.sum(kr * kr, axis=-1, keepdims=True)
    kr_roped = rope(kr * gk[:, QK_NOPE:])
    for h in range(n_heads):
        lo = h * HEAD_SLOT
        slab = qx[:, lo:lo + HEAD_SLOT]
        r = lax.rsqrt(jnp.sum(slab * slab, axis=-1, keepdims=True) * inv_w + EPS)
        q_ref[:, lo:lo + QK_NOPE] = (slab[:, :QK_NOPE] * r * gq[:, :QK_NOPE]).astype(BF16)
        q_ref[:, lo + QK_NOPE:lo + HEAD_SLOT] = rope(slab[:, QK_NOPE:] * r * gq[:, QK_NOPE:]).astype(BF16)
        kn = kvx[:, h * QK_NOPE:(h + 1) * QK_NOPE]
        rk = lax.rsqrt((jnp.sum(kn * kn, axis=-1, keepdims=True) + kr_ss) * inv_w + EPS)
        k_ref[:, lo:lo + QK_NOPE] = (kn * rk * gk[:, :QK_NOPE]).astype(BF16)
        k_ref[:, lo + QK_NOPE:lo + HEAD_SLOT] = (kr_roped * rk).astype(BF16)
    v_ref[...] = kvx[:, n_heads * QK_NOPE:].astype(BF16)


def _mix_in(x2, pos2, g, win, gq, gkv, wuq, wukv, gqh, gkh, invf, *, pool_w, q_lora, kv_lora, n_heads, tm):
    t, d = x2.shape
    kern = functools.partial(_mix_in_kernel, pool_w=pool_w, q_lora=q_lora, kv_lora=kv_lora, n_heads=n_heads)
    row = lambda w: pl.BlockSpec((tm, w), lambda i: (i, 0))
    return pl.pallas_call(
        kern,
        grid=(t // tm,),
        in_specs=[row(d), row(1), _const_spec(g.shape), _const_spec(win.shape), _const_spec(gq.shape),
                  _const_spec(gkv.shape), _const_spec(wuq.shape), _const_spec(wukv.shape), _const_spec(gqh.shape),
                  _const_spec(gkh.shape), _const_spec(invf.shape)],
        out_specs=[row(pool_w), row(n_heads * HEAD_SLOT), row(n_heads * HEAD_SLOT), row(n_heads * V_HEAD)],
        out_shape=[jax.ShapeDtypeStruct((t, pool_w), F32),
                   jax.ShapeDtypeStruct((t, n_heads * HEAD_SLOT), BF16),
                   jax.ShapeDtypeStruct((t, n_heads * HEAD_SLOT), BF16),
                   jax.ShapeDtypeStruct((t, n_heads * V_HEAD), BF16)],
        compiler_params=_params("parallel"),
        name="mix_in",
    )(x2, pos2, g, win, gq, gkv, wuq, wukv, gqh, gkh, invf)


def _attn_kernel(q_ref, k_ref, v_ref, o_ref, *, tq):
    qi = pl.program_id(2)
    q = q_ref[...]
    q_chunk = (qi * tq + lax.broadcasted_iota(jnp.int32, (tq, 1), 0)) // CHUNK

    def body(j, carry):
        m, l, acc = carry
        start = pl.multiple_of(j * tq, tq)
        s = _dot_nt(q, k_ref[pl.ds(start, tq), :])
        k_chunk = (start + lax.broadcasted_iota(jnp.int32, (1, tq), 1)) // CHUNK
        s = jnp.where(k_chunk <= q_chunk, s, -jnp.inf)
        m_new = jnp.maximum(m, jnp.max(s, axis=-1, keepdims=True))
        alpha = jnp.exp(m - m_new)
        p = jnp.exp(s - m_new)
        l = alpha * l + jnp.sum(p, axis=-1, keepdims=True)
        acc = alpha * acc + _dot(p.astype(BF16), v_ref[pl.ds(start, tq), :])
        return m_new, l, acc

    init = (jnp.full((tq, 1), -jnp.inf, F32), jnp.zeros((tq, 1), F32), jnp.zeros((tq, V_HEAD), F32))
    _, l, acc = lax.fori_loop(0, qi + 1, body, init)
    o_ref[...] = (acc / l).astype(BF16)


def _attention(q3, k3, v3, *, n_heads, tq):
    b, s, _ = q3.shape
    return pl.pallas_call(
        functools.partial(_attn_kernel, tq=tq),
        grid=(b, n_heads, s // tq),
        in_specs=[pl.BlockSpec((None, tq, HEAD_SLOT), lambda bi, h, i: (bi, i, h)),
                  pl.BlockSpec((None, s, HEAD_SLOT), lambda bi, h, i: (bi, 0, h)),
                  pl.BlockSpec((None, s, V_HEAD), lambda bi, h, i: (bi, 0, h))],
        out_specs=pl.BlockSpec((None, tq, V_HEAD), lambda bi, h, i: (bi, i, h)),
        out_shape=jax.ShapeDtypeStruct((b, s, n_heads * V_HEAD), BF16),
        compiler_params=_params("parallel", "parallel", "arbitrary"),
        name="attn",
    )(q3, k3, v3)


def _mix_out_kernel(zp_ref, halo_ref, o_ref, x_ref, wpool_ref, pscale_ref, wout_ref, h_ref, ext_ref, *, seq, tm):
    t0 = (pl.program_id(0) * tm) % seq
    pool_w = zp_ref.shape[1]
    gw = pool_w // len(POOL_WINDOWS)
    ext_ref[0:POOL_HALO, :] = jnp.where(t0 == 0, 0.0, halo_ref[...])
    ext_ref[POOL_HALO:, :] = zp_ref[...]
    tpos = t0 + lax.broadcasted_iota(jnp.int32, (tm, 1), 0)
    acc = x_ref[...] + _dot(o_ref[...], wout_ref[pool_w:, :])
    for g, w in enumerate(POOL_WINDOWS):
        lo = g * gw
        cur = ext_ref[POOL_HALO:POOL_HALO + tm, lo:lo + gw]
        win_sum = cur
        for j in range(1, w):
            win_sum = win_sum + ext_ref[POOL_HALO - j:POOL_HALO - j + tm, lo:lo + gw]
        cnt = jnp.minimum(tpos + 1, w).astype(F32)
        dlt = win_sum / cnt - cur
        y = _dot(dlt.astype(BF16), wpool_ref[g]) * pscale_ref[:, lo:lo + gw]
        acc = acc + _dot(y.astype(BF16), wout_ref[lo:lo + gw, :])
    h_ref[...] = acc


def _mix_out(zp, o2, x2, wpool, pscale, wout, *, seq, tm):
    t, d = x2.shape
    pool_w = zp.shape[1]
    hb = tm // POOL_HALO
    row = lambda w: pl.BlockSpec((tm, w), lambda i: (i, 0))
    return pl.pallas_call(
        functools.partial(_mix_out_kernel, seq=seq, tm=tm),
        grid=(t // tm,),
        in_specs=[row(pool_w),
                  pl.BlockSpec((POOL_HALO, pool_w), lambda i: (jnp.maximum(i * hb - 1, 0), 0)),
                  row(o2.shape[1]), row(d), _const_spec(wpool.shape), _const_spec(pscale.shape),
                  _const_spec(wout.shape)],
        out_specs=row(d),
        out_shape=jax.ShapeDtypeStruct((t, d), F32),
        scratch_shapes=[pltpu.VMEM((tm + POOL_HALO, pool_w), F32)],
        compiler_params=_params("parallel"),
        name="mix_out",
    )(zp, zp, o2, x2, wpool, pscale, wout)


def _topk_cols(s, n):
    tm = s.shape[1]
    rows = lax.broadcasted_iota(jnp.int32, s.shape, 0).astype(F32)
    slot = lax.broadcasted_iota(jnp.int32, (PEER_TOPK, tm), 0)
    vals = jnp.zeros((PEER_TOPK, tm), F32)
    idxs = jnp.zeros((PEER_TOPK, tm), F32)
    for k in range(PEER_TOPK):
        m = jnp.max(s, axis=0, keepdims=True)
        i = jnp.min(jnp.where(s == m, rows, float(n)), axis=0, keepdims=True)
        vals = jnp.where(slot == k, m, vals)
        idxs = jnp.where(slot == k, i, idxs)
        s = jnp.where(rows == i, -jnp.inf, s)
    return vals, idxs


def _route_kernel(h_ref, g_ref, wpq_ref, k1_ref, k2_ref, xn_ref, idx_ref, gate_ref, qp_ref, idxt_ref, gatet_ref):
    xn = _rms(h_ref[...]) * g_ref[...]
    xn_ref[...] = xn
    qp_ref[...] = _dot(xn.astype(BF16), wpq_ref[...]).astype(BF16)

    def head(hd, carry):
        off = pl.multiple_of(hd * 2 * PEER_HALF, 2 * PEER_HALF)
        s1 = _dot_nt(k1_ref[hd], qp_ref[:, pl.ds(off, PEER_HALF)])
        s2 = _dot_nt(k2_ref[hd], qp_ref[:, pl.ds(off + PEER_HALF, PEER_HALF)])
        v1, i1 = _topk_cols(s1, PEER_NKEYS)
        v2, i2 = _topk_cols(s2, PEER_NKEYS)
        cand = jnp.concatenate([v1[a:a + 1, :] + v2 for a in range(PEER_TOPK)], axis=0)
        vs, ci = _topk_cols(cand, PEER_TOPK * PEER_TOPK)
        ci = ci.astype(jnp.int32)
        a_sel = ci // PEER_TOPK
        b_sel = ci % PEER_TOPK
        e1 = jnp.zeros_like(vs)
        e2 = jnp.zeros_like(vs)
        for a in range(PEER_TOPK):
            e1 = jnp.where(a_sel == a, i1[a:a + 1, :], e1)
            e2 = jnp.where(b_sel == a, i2[a:a + 1, :], e2)
        ex = jnp.exp(vs - jnp.max(vs, axis=0, keepdims=True))
        r0 = pl.multiple_of(hd * PEER_TOPK, PEER_TOPK)
        idxt_ref[pl.ds(r0, PEER_TOPK), :] = (e1 * PEER_NKEYS + e2).astype(jnp.int32)
        gatet_ref[pl.ds(r0, PEER_TOPK), :] = ex / jnp.sum(ex, axis=0, keepdims=True)
        return carry

    lax.fori_loop(0, PEER_HEADS, head, 0)
    idx_ref[...] = idxt_ref[...].T
    gate_ref[...] = gatet_ref[...].T


def _route(h1, g, wpq, k1, k2, *, tm):
    t, d = h1.shape
    row = lambda w: pl.BlockSpec((tm, w), lambda i: (i, 0))
    return pl.pallas_call(
        _route_kernel,
        grid=(t // tm,),
        in_specs=[row(d), _const_spec(g.shape), _const_spec(wpq.shape), _const_spec(k1.shape),
                  _const_spec(k2.shape)],
        out_specs=[row(d), row(PEER_SLOTS), row(PEER_SLOTS)],
        out_shape=[jax.ShapeDtypeStruct((t, d), F32),
                   jax.ShapeDtypeStruct((t, PEER_SLOTS), jnp.int32),
                   jax.ShapeDtypeStruct((t, PEER_SLOTS), F32)],
        scratch_shapes=[pltpu.VMEM((tm, wpq.shape[1]), BF16),
                        pltpu.VMEM((PEER_SLOTS, tm), jnp.int32),
                        pltpu.VMEM((PEER_SLOTS, tm), F32)],
        compiler_params=_params("parallel"),
        name="peer_route",
    )(h1, g, wpq, k1, k2)


def _experts_kernel(idx_hbm, xn_ref, gate_ref, h_ref, tab_hbm, out_ref, idx_smem, buf, sem, idx_sem, *, tm, d):
    i = pl.program_id(0)
    idx_cp = pltpu.make_async_copy(idx_hbm.at[pl.ds(i * tm, tm), :], idx_smem, idx_sem.at[0])
    idx_cp.start()
    idx_cp.wait()

    def issue(t, slot):
        for k in range(PEER_SLOTS):
            e = idx_smem[t, k]
            pltpu.make_async_copy(tab_hbm.at[pl.ds(e, 1), :], buf.at[slot, pl.ds(k, 1), :], sem.at[slot]).start()

    def wait(slot):
        pltpu.make_async_copy(tab_hbm.at[pl.ds(0, PEER_SLOTS), :], buf.at[slot], sem.at[slot]).wait()

    for s in range(GATHER_RING - 1):
        issue(s, s)

    sub = lax.broadcasted_iota(jnp.int32, (8, PEER_SLOTS), 0)

    def group(gi, carry):
        base = pl.multiple_of(gi * 8, 8)
        x8 = xn_ref[pl.ds(base, 8), :].astype(BF16)
        g8 = gate_ref[pl.ds(base, 8), :]
        acc = h_ref[pl.ds(base, 8), :]
        for j in range(8):
            nxt = base + j + GATHER_RING - 1

            @pl.when(nxt < tm)
            def _():
                issue(nxt, (j + GATHER_RING - 1) % GATHER_RING)

            slot = j % GATHER_RING
            wait(slot)
            u = buf[slot, :, :d].astype(BF16)
            v = buf[slot, :, d:].astype(BF16)
            act = _dot_nt(x8, u)
            w = jnp.where(sub == j, jax.nn.gelu(act) * g8, 0.0)
            acc = acc + _dot(w.astype(BF16), v)
        out_ref[pl.ds(base, 8), :] = acc
        return carry

    lax.fori_loop(0, tm // 8, group, 0)


def _experts(idx, xn, gates, h1, table, *, tm):
    t, d = xn.shape
    row = lambda w: pl.BlockSpec((tm, w), lambda i: (i, 0))
    return pl.pallas_call(
        functools.partial(_experts_kernel, tm=tm, d=d),
        grid=(t // tm,),
        in_specs=[pl.BlockSpec(memory_space=pl.ANY), row(d), row(PEER_SLOTS), row(d),
                  pl.BlockSpec(memory_space=pl.ANY)],
        out_specs=row(d),
        out_shape=jax.ShapeDtypeStruct((t, d), F32),
        scratch_shapes=[pltpu.SMEM((tm, PEER_SLOTS), jnp.int32),
                        pltpu.VMEM((GATHER_RING, PEER_SLOTS, 2 * d), F32),
                        pltpu.SemaphoreType.DMA((GATHER_RING,)),
                        pltpu.SemaphoreType.DMA((1,))],
        compiler_params=_params("arbitrary"),
        name="peer_experts",
    )(idx, xn, gates, h1, table)


def _ple_kernel(h_ref, p_ref, g_ref, wg_ref, wp_ref, o_ref):
    h = h_ref[...]
    xn = _rms(h) * g_ref[...]
    gate = jax.nn.sigmoid(_dot(xn.astype(BF16), wg_ref[...]))
    o_ref[...] = h + gate * _dot(p_ref[...].astype(BF16), wp_ref[...])


def _ple(h2, p2, g, wg, wp, *, tm):
    t, d = h2.shape
    row = lambda w: pl.BlockSpec((tm, w), lambda i: (i, 0))
    return pl.pallas_call(
        _ple_kernel,
        grid=(t // tm,),
        in_specs=[row(d), row(p2.shape[1]), _const_spec(g.shape), _const_spec(wg.shape), _const_spec(wp.shape)],
        out_specs=row(d),
        out_shape=jax.ShapeDtypeStruct((t, d), F32),
        compiler_params=_params("parallel"),
        name="ple",
    )(h2, p2, g, wg, wp)


def _layer(h2, p2, pos2, batch, seq, mix_norm_gain, w_in, w_pool, pool_scale, q_lat_gain, kv_lat_gain, w_uq, w_ukv,
           q_norm_gain, k_norm_gain, w_out, ffn_norm_gain, w_pq, sub_k1, sub_k2, expert_u, expert_v, ple_norm_gain,
           w_ple_gate, w_ple_proj):
    t, d = h2.shape
    q_lora = q_lat_gain.shape[0]
    kv_lora = kv_lat_gain.shape[0]
    n_heads = w_uq.shape[1] // QK_HEAD
    pool_w = w_in.shape[1] - q_lora - kv_lora - QK_ROPE
    rowv = lambda a: a.reshape(1, -1).astype(F32)

    win = jnp.pad(w_in, ((0, 0), (0, LANES - QK_ROPE))).astype(BF16)
    wuq = jnp.pad(w_uq.reshape(q_lora, n_heads, QK_HEAD), ((0, 0), (0, 0), (0, HEAD_SLOT - QK_HEAD)))
    wuq = wuq.reshape(q_lora, n_heads * HEAD_SLOT).astype(BF16)
    wukv = w_ukv.reshape(kv_lora, n_heads, 2, QK_NOPE).transpose(0, 2, 1, 3).reshape(kv_lora, -1).astype(BF16)
    pad_gain = lambda g: jnp.pad(g, (0, HEAD_SLOT - QK_HEAD)).reshape(1, HEAD_SLOT).astype(F32)
    gqh = pad_gain(q_norm_gain) * (QK_HEAD ** -0.5)
    gkh = pad_gain(k_norm_gain)
    inv_freq = ROPE_THETA ** (-jnp.arange(0, QK_ROPE, 2, dtype=F32) / QK_ROPE)
    invf = jnp.concatenate([inv_freq, inv_freq, jnp.zeros((LANES - QK_ROPE,), F32)]).reshape(1, LANES)

    zp, q, k, v = _mix_in(h2, pos2, rowv(mix_norm_gain), win, rowv(q_lat_gain), rowv(kv_lat_gain), wuq, wukv, gqh,
                          gkh, invf, pool_w=pool_w, q_lora=q_lora, kv_lora=kv_lora, n_heads=n_heads, tm=256)
    shp = lambda a: a.reshape(batch, seq, a.shape[1])
    o = _attention(shp(q), shp(k), shp(v), n_heads=n_heads, tq=256)
    h1 = _mix_out(zp, o.reshape(t, -1), h2, w_pool.astype(BF16), rowv(pool_scale), w_out.astype(BF16), seq=seq,
                  tm=256)
    xn, idx, gates = _route(h1, rowv(ffn_norm_gain), w_pq.astype(BF16), sub_k1.astype(BF16), sub_k2.astype(BF16),
                            tm=256)
    table = jnp.concatenate([expert_u, expert_v], axis=1)
    h3 = _experts(idx, xn, gates, h1, table, tm=128)
    return _ple(h3, p2, rowv(ple_norm_gain), w_ple_gate.astype(BF16), w_ple_proj.astype(BF16), tm=256)


def kernel(x, p, positions, mix_norm_gain, w_in, w_pool, pool_scale, q_lat_gain, kv_lat_gain, w_uq, w_ukv,
           q_norm_gain, k_norm_gain, w_out, ffn_norm_gain, w_pq, sub_k1, sub_k2, expert_u, expert_v, ple_norm_gain,
           w_ple_gate, w_ple_proj):
    batch, seq, d = x.shape
    t = batch * seq
    h = x.reshape(t, d)
    pos2 = positions.reshape(t, 1).astype(jnp.int32)
    for i in range(w_in.shape[0]):
        h = _layer(h, p[i].reshape(t, -1), pos2, batch, seq, mix_norm_gain[i], w_in[i], w_pool[i], pool_scale[i],
                   q_lat_gain[i], kv_lat_gain[i], w_uq[i], w_ukv[i], q_norm_gain[i], k_norm_gain[i], w_out[i],
                   ffn_norm_gain[i], w_pq[i], sub_k1[i], sub_k2[i], expert_u[i], expert_v[i], ple_norm_gain[i],
                   w_ple_gate[i], w_ple_proj[i])
    return h.reshape(batch, seq, d)
```

```python
import functools

import jax
import jax.numpy as jnp
from jax import lax
from jax.experimental import pallas as pl
from jax.experimental.pallas import tpu as pltpu

F32 = jnp.float32
BF16 = jnp.bfloat16

EPS = 1e-6
CHUNK = 64
POOL_WINDOWS = (2, 4, 8, 16)
POOL_HALO = 16
V_HEAD = 128
QK_NOPE = 128
QK_ROPE = 64
QK_HEAD = QK_NOPE + QK_ROPE
HEAD_SLOT = 256
ROPE_THETA = 10000.0
PEER_HEADS = 8
PEER_NKEYS = 128
PEER_HALF = 128
PEER_TOPK = 16
PEER_SLOTS = PEER_HEADS * PEER_TOPK

LANES = 128
VMEM_LIMIT_BYTES = 56 * 1024 * 1024

OCTET = 8
GATHER_GROUP = 4
GROUPS_PER_OCTET = OCTET // GATHER_GROUP


def _rms(x):
    return x * lax.rsqrt(jnp.mean(x * x, axis=-1, keepdims=True) + EPS)


def _dot(a, b):
    return jnp.dot(a, b, preferred_element_type=F32)


def _dot_nt(a, b):
    return lax.dot_general(a, b, (((1,), (1,)), ((), ())), preferred_element_type=F32)


def _params(*sem):
    return pltpu.CompilerParams(dimension_semantics=sem, vmem_limit_bytes=VMEM_LIMIT_BYTES)


def _const_spec(shape):
    nd = len(shape)
    return pl.BlockSpec(shape, lambda *_: (0,) * nd)


def _mix_in_kernel(x_ref, pos_ref, g_ref, win_ref, gq_ref, gkv_ref, wuq_ref, wukv_ref, gqh_ref, gkh_ref,
                   invf_ref, zp_ref, q_ref, k_ref, v_ref, *, pool_w, q_lora, kv_lora, n_heads):
    xn = _rms(x_ref[...]) * g_ref[...]
    z = _dot(xn.astype(BF16), win_ref[...])
    zp_ref[...] = z[:, :pool_w]
    c1 = pool_w + q_lora
    c2 = c1 + kv_lora
    cq = _rms(z[:, pool_w:c1]) * gq_ref[...]
    ckv = _rms(z[:, c1:c2]) * gkv_ref[...]
    kr = z[:, c2:c2 + LANES]
    qx = _dot(cq.astype(BF16), wuq_ref[...])
    kvx = _dot(ckv.astype(BF16), wukv_ref[...])

    ang = pos_ref[...].astype(F32) * invf_ref[...]
    cos = jnp.cos(ang)
    sin = jnp.sin(ang)
    lane = lax.broadcasted_iota(jnp.int32, ang.shape, 1)
    first_half = lane < QK_ROPE // 2
    sin_signed = jnp.where(first_half, -sin, sin)

    def rope(y):
        partner = jnp.where(first_half, pltpu.roll(y, LANES - QK_ROPE // 2, 1), pltpu.roll(y, QK_ROPE // 2, 1))
        return y * cos + partner * sin_signed

    gq = gqh_ref[...]
    gk = gkh_ref[...]
    inv_w = 1.0 / QK_HEAD
    kr_ss = jnp.sum(kr * kr, axis=-1, keepdims=True)
    kr_roped = rope(kr * gk[:, QK_NOPE:])
    for h in range(n_heads):
        lo = h * HEAD_SLOT
        slab = qx[:, lo:lo + HEAD_SLOT]
        r = lax.rsqrt(jnp.sum(slab * slab, axis=-1, keepdims=True) * inv_w + EPS)
        q_ref[:, lo:lo + QK_NOPE] = (slab[:, :QK_NOPE] * r * gq[:, :QK_NOPE]).astype(BF16)
        q_ref[:, lo + QK_NOPE:lo + HEAD_SLOT] = rope(slab[:, QK_NOPE:] * r * gq[:, QK_NOPE:]).astype(BF16)
        kn = kvx[:, h * QK_NOPE:(h + 1) * QK_NOPE]
        rk = lax.rsqrt((jnp.sum(kn * kn, axis=-1, keepdims=True) + kr_ss) * inv_w + EPS)
        k_ref[:, lo:lo + QK_NOPE] = (kn * rk * gk[:, :QK_NOPE]).astype(BF16)
        k_ref[:, lo + QK_NOPE:lo + HEAD_SLOT] = (kr_roped * rk).astype(BF16)
    v_ref[...] = kvx[:, n_heads * QK_NOPE:].astype(BF16)


def _mix_in(x2, pos2, g, win, gq, gkv, wuq, wukv, gqh, gkh, invf, *, pool_w, q_lora, kv_lora, n_heads, tm):
    t, d = x2.shape
    kern = functools.partial(_mix_in_kernel, pool_w=pool_w, q_lora=q_lora, kv_lora=kv_lora, n_heads=n_heads)
    row = lambda w: pl.BlockSpec((tm, w), lambda i: (i, 0))
    return pl.pallas_call(
        kern,
        grid=(t // tm,),
        in_specs=[row(d), row(1), _const_spec(g.shape), _const_spec(win.shape), _const_spec(gq.shape),
                  _const_spec(gkv.shape), _const_spec(wuq.shape), _const_spec(wukv.shape), _const_spec(gqh.shape),
                  _const_spec(gkh.shape), _const_spec(invf.shape)],
        out_specs=[row(pool_w), row(n_heads * HEAD_SLOT), row(n_heads * HEAD_SLOT), row(n_heads * V_HEAD)],
        out_shape=[jax.ShapeDtypeStruct((t, pool_w), F32),
                   jax.ShapeDtypeStruct((t, n_heads * HEAD_SLOT), BF16),
                   jax.ShapeDtypeStruct((t, n_heads * HEAD_SLOT), BF16),
                   jax.ShapeDtypeStruct((t, n_heads * V_HEAD), BF16)],
        compiler_params=_params("parallel"),
        name="mix_in",
    )(x2, pos2, g, win, gq, gkv, wuq, wukv, gqh, gkh, invf)


def _attn_kernel(q_ref, k_ref, v_ref, o_ref, *, tq):
    qi = pl.program_id(2)
    q = q_ref[...]
    q_chunk = (qi * tq + lax.broadcasted_iota(jnp.int32, (tq, 1), 0)) // CHUNK

    def body(j, carry):
        m, l, acc = carry
        start = pl.multiple_of(j * tq, tq)
        s = _dot_nt(q, k_ref[pl.ds(start, tq), :])
        k_chunk = (start + lax.broadcasted_iota(jnp.int32, (1, tq), 1)) // CHUNK
        s = jnp.where(k_chunk <= q_chunk, s, -jnp.inf)
        m_new = jnp.maximum(m, jnp.max(s, axis=-1, keepdims=True))
        alpha = jnp.exp(m - m_new)
        p = jnp.exp(s - m_new)
        l = alpha * l + jnp.sum(p, axis=-1, keepdims=True)
        acc = alpha * acc + _dot(p.astype(BF16), v_ref[pl.ds(start, tq), :])
        return m_new, l, acc

    init = (jnp.full((tq, 1), -jnp.inf, F32), jnp.zeros((tq, 1), F32), jnp.zeros((tq, V_HEAD), F32))
    _, l, acc = lax.fori_loop(0, qi + 1, body, init)
    o_ref[...] = (acc / l).astype(BF16)


def _attention(q3, k3, v3, *, n_heads, tq):
    b, s, _ = q3.shape
    return pl.pallas_call(
        functools.partial(_attn_kernel, tq=tq),
        grid=(b, n_heads, s // tq),
        in_specs=[pl.BlockSpec((None, tq, HEAD_SLOT), lambda bi, h, i: (bi, i, h)),
                  pl.BlockSpec((None, s, HEAD_SLOT), lambda bi, h, i: (bi, 0, h)),
                  pl.BlockSpec((None, s, V_HEAD), lambda bi, h, i: (bi, 0, h))],
        out_specs=pl.BlockSpec((None, tq, V_HEAD), lambda bi, h, i: (bi, i, h)),
        out_shape=jax.ShapeDtypeStruct((b, s, n_heads * V_HEAD), BF16),
        compiler_params=_params("parallel", "parallel", "arbitrary"),
        name="attn",
    )(q3, k3, v3)


def _mix_out_kernel(zp_ref, halo_ref, o_ref, x_ref, wpool_ref, pscale_ref, wout_ref, h_ref, ext_ref, *, seq, tm):
    t0 = (pl.program_id(0) * tm) % seq
    pool_w = zp_ref.shape[1]
    gw = pool_w // len(POOL_WINDOWS)
    ext_ref[0:POOL_HALO, :] = jnp.where(t0 == 0, 0.0, halo_ref[...])
    ext_ref[POOL_HALO:, :] = zp_ref[...]
    tpos = t0 + lax.broadcasted_iota(jnp.int32, (tm, 1), 0)
    acc = x_ref[...] + _dot(o_ref[...], wout_ref[pool_w:, :])
    for g, w in enumerate(POOL_WINDOWS):
        lo = g * gw
        cur = ext_ref[POOL_HALO:POOL_HALO + tm, lo:lo + gw]
        win_sum = cur
        for j in range(1, w):
            win_sum = win_sum + ext_ref[POOL_HALO - j:POOL_HALO - j + tm, lo:lo + gw]
        cnt = jnp.minimum(tpos + 1, w).astype(F32)
        dlt = win_sum / cnt - cur
        y = _dot(dlt.astype(BF16), wpool_ref[g]) * pscale_ref[:, lo:lo + gw]
        acc = acc + _dot(y.astype(BF16), wout_ref[lo:lo + gw, :])
    h_ref[...] = acc


def _mix_out(zp, o2, x2, wpool, pscale, wout, *, seq, tm):
    t, d = x2.shape
    pool_w = zp.shape[1]
    hb = tm // POOL_HALO
    row = lambda w: pl.BlockSpec((tm, w), lambda i: (i, 0))
    return pl.pallas_call(
        functools.partial(_mix_out_kernel, seq=seq, tm=tm),
        grid=(t // tm,),
        in_specs=[row(pool_w),
                  pl.BlockSpec((POOL_HALO, pool_w), lambda i: (jnp.maximum(i * hb - 1, 0), 0)),
                  row(o2.shape[1]), row(d), _const_spec(wpool.shape), _const_spec(pscale.shape),
                  _const_spec(wout.shape)],
        out_specs=row(d),
        out_shape=jax.ShapeDtypeStruct((t, d), F32),
        scratch_shapes=[pltpu.VMEM((tm + POOL_HALO, pool_w), F32)],
        compiler_params=_params("parallel"),
        name="mix_out",
    )(zp, zp, o2, x2, wpool, pscale, wout)


def _topk_cols(s, n):
    tm = s.shape[1]
    rows = lax.broadcasted_iota(jnp.int32, s.shape, 0).astype(F32)
    slot = lax.broadcasted_iota(jnp.int32, (PEER_TOPK, tm), 0)
    vals = jnp.zeros((PEER_TOPK, tm), F32)
    idxs = jnp.zeros((PEER_TOPK, tm), F32)
    for k in range(PEER_TOPK):
        m = jnp.max(s, axis=0, keepdims=True)
        i = jnp.min(jnp.where(s == m, rows, float(n)), axis=0, keepdims=True)
        vals = jnp.where(slot == k, m, vals)
        idxs = jnp.where(slot == k, i, idxs)
        s = jnp.where(rows == i, -jnp.inf, s)
    return vals, idxs


def _route_kernel(h_ref, g_ref, wpq_ref, k1_ref, k2_ref, xn_ref, idx_ref, gate_ref, qp_ref, idxt_ref, gatet_ref):
    xn = _rms(h_ref[...]) * g_ref[...]
    xn_ref[...] = xn
    qp_ref[...] = _dot(xn.astype(BF16), wpq_ref[...]).astype(BF16)

    def head(hd, carry):
        off = pl.multiple_of(hd * 2 * PEER_HALF, 2 * PEER_HALF)
        s1 = _dot_nt(k1_ref[hd], qp_ref[:, pl.ds(off, PEER_HALF)])
        s2 = _dot_nt(k2_ref[hd], qp_ref[:, pl.ds(off + PEER_HALF, PEER_HALF)])
        v1, i1 = _topk_cols(s1, PEER_NKEYS)
        v2, i2 = _topk_cols(s2, PEER_NKEYS)
        cand = jnp.concatenate([v1[a:a + 1, :] + v2 for a in range(PEER_TOPK)], axis=0)
        vs, ci = _topk_cols(cand, PEER_TOPK * PEER_TOPK)
        ci = ci.astype(jnp.int32)
        a_sel = ci // PEER_TOPK
        b_sel = ci % PEER_TOPK
        e1 = jnp.zeros_like(vs)
        e2 = jnp.zeros_like(vs)
        for a in range(PEER_TOPK):
            e1 = jnp.where(a_sel == a, i1[a:a + 1, :], e1)
            e2 = jnp.where(b_sel == a, i2[a:a + 1, :], e2)
        ex = jnp.exp(vs - jnp.max(vs, axis=0, keepdims=True))
        r0 = pl.multiple_of(hd * PEER_TOPK, PEER_TOPK)
        idxt_ref[pl.ds(r0, PEER_TOPK), :] = (e1 * PEER_NKEYS + e2).astype(jnp.int32)
        gatet_ref[pl.ds(r0, PEER_TOPK), :] = ex / jnp.sum(ex, axis=0, keepdims=True)
        return carry

    lax.fori_loop(0, PEER_HEADS, head, 0)
    idx_ref[...] = idxt_ref[...].T
    gate_ref[...] = gatet_ref[...].T


def _route(h1, g, wpq, k1, k2, *, tm):
    t, d = h1.shape
    row = lambda w: pl.BlockSpec((tm, w), lambda i: (i, 0))
    return pl.pallas_call(
        _route_kernel,
        grid=(t // tm,),
        in_specs=[row(d), _const_spec(g.shape), _const_spec(wpq.shape), _const_spec(k1.shape),
                  _const_spec(k2.shape)],
        out_specs=[row(d), row(PEER_SLOTS), row(PEER_SLOTS)],
        out_shape=[jax.ShapeDtypeStruct((t, d), F32),
                   jax.ShapeDtypeStruct((t, PEER_SLOTS), jnp.int32),
                   jax.ShapeDtypeStruct((t, PEER_SLOTS), F32)],
        scratch_shapes=[pltpu.VMEM((tm, wpq.shape[1]), BF16),
                        pltpu.VMEM((PEER_SLOTS, tm), jnp.int32),
                        pltpu.VMEM((PEER_SLOTS, tm), F32)],
        compiler_params=_params("parallel"),
        name="peer_route",
    )(h1, g, wpq, k1, k2)


def _experts_kernel(idx_hbm, xn_ref, gate_ref, h_ref, expand_ref, tab_hbm, out_ref, idx_smem, buf_a, buf_b, sem,
                    idx_sem, *, tm):
    bufs = (buf_a, buf_b)
    n_chunk = buf_a.shape[2]
    n_oct = tm // OCTET
    i = pl.program_id(0)
    idx_cp = pltpu.make_async_copy(idx_hbm.at[pl.ds(i * tm, tm), :], idx_smem, idx_sem.at[0])
    idx_cp.start()
    idx_cp.wait()

    def issue(o, gg, half):
        for tt in range(GATHER_GROUP):
            t = o * OCTET + gg * GATHER_GROUP + tt
            for k in range(PEER_SLOTS):
                pltpu.make_async_copy(tab_hbm.at[idx_smem[t, k]], bufs[half].at[gg, tt, :, pl.ds(k, 1), :],
                                      sem.at[half, gg]).start(priority=k % 2)

    def wait(gg, half):
        pltpu.make_async_copy(bufs[half].at[gg], bufs[half].at[gg], sem.at[half, gg]).wait()

    lane = lax.broadcasted_iota(jnp.int32, (PEER_SLOTS, LANES), 1)
    sub = lax.broadcasted_iota(jnp.int32, (OCTET, 2 * PEER_SLOTS), 0)

    def octet(o, half, prefetch):
        buf = bufs[half]
        base = pl.multiple_of(o * OCTET, OCTET)
        cols = jnp.zeros((PEER_SLOTS, LANES), F32)
        x8 = xn_ref[pl.ds(base, OCTET), :]
        for gg in range(GROUPS_PER_OCTET):
            wait(gg, half)
            if prefetch:
                issue(o + 1, gg, 1 - half)
            for tt in range(GATHER_GROUP):
                j = gg * GATHER_GROUP + tt
                part = jnp.zeros((PEER_SLOTS, LANES), F32)
                for c in range(n_chunk):
                    u = pltpu.bitcast(buf[gg, tt, c] << 16, F32)
                    part = part + u * x8[j:j + 1, c * LANES:(c + 1) * LANES]
                cols = jnp.where(lane == j, jnp.sum(part, axis=-1, keepdims=True), cols)
        act = cols.T[:OCTET, :]
        wgt = jax.nn.gelu(act) * gate_ref[pl.ds(base, OCTET), :]
        wgt = _dot(wgt.astype(BF16), expand_ref[...])
        acc = h_ref[pl.ds(base, OCTET), :]
        for gg in range(GROUPS_PER_OCTET):
            for tt in range(GATHER_GROUP):
                j = gg * GATHER_GROUP + tt
                wj = jnp.where(sub == j, wgt, 0.0).astype(BF16)
                acc = acc + jnp.concatenate(
                    [_dot(wj, pltpu.bitcast(buf[gg, tt, c], BF16)) for c in range(n_chunk)], axis=1)
        out_ref[pl.ds(base, OCTET), :] = acc

    for gg in range(GROUPS_PER_OCTET):
        issue(0, gg, 0)

    def octet_pair(pi, carry):
        octet(2 * pi, 0, True)
        octet(2 * pi + 1, 1, True)
        return carry

    lax.fori_loop(0, n_oct // 2 - 1, octet_pair, 0)
    octet(n_oct - 2, 0, True)
    octet(n_oct - 1, 1, False)


def _experts(idx, xn, gates, h1, table, *, tm):
    t, d = xn.shape
    row = lambda w: pl.BlockSpec((tm, w), lambda i: (i, 0))
    slot_ids = jnp.arange(PEER_SLOTS)
    expand = jnp.zeros((PEER_SLOTS, 2 * PEER_SLOTS), BF16).at[slot_ids, 2 * slot_ids + 1].set(1)
    return pl.pallas_call(
        functools.partial(_experts_kernel, tm=tm),
        grid=(t // tm,),
        in_specs=[pl.BlockSpec(memory_space=pl.ANY), row(d), row(PEER_SLOTS), row(d), _const_spec(expand.shape),
                  pl.BlockSpec(memory_space=pl.ANY)],
        out_specs=row(d),
        out_shape=jax.ShapeDtypeStruct((t, d), F32),
        scratch_shapes=[pltpu.SMEM((tm, PEER_SLOTS), jnp.int32),
                        pltpu.VMEM((GROUPS_PER_OCTET, GATHER_GROUP, d // LANES, PEER_SLOTS, LANES), jnp.uint32),
                        pltpu.VMEM((GROUPS_PER_OCTET, GATHER_GROUP, d // LANES, PEER_SLOTS, LANES), jnp.uint32),
                        pltpu.SemaphoreType.DMA((2, GROUPS_PER_OCTET)),
                        pltpu.SemaphoreType.DMA((1,))],
        compiler_params=_params("arbitrary"),
        name="peer_experts",
    )(idx, xn, gates, h1, expand, table)


def _pack_expert_table(expert_u, expert_v):
    e, d = expert_u.shape
    bits = lambda a: lax.bitcast_convert_type(a.astype(BF16), jnp.uint16).astype(jnp.uint32)
    words = bits(expert_u) | (bits(expert_v) << 16)
    return words.reshape(e, d // LANES, 1, LANES)


def _ple_kernel(h_ref, p_ref, g_ref, wg_ref, wp_ref, o_ref):
    h = h_ref[...]
    xn = _rms(h) * g_ref[...]
    gate = jax.nn.sigmoid(_dot(xn.astype(BF16), wg_ref[...]))
    o_ref[...] = h + gate * _dot(p_ref[...].astype(BF16), wp_ref[...])


def _ple(h2, p2, g, wg, wp, *, tm):
    t, d = h2.shape
    row = lambda w: pl.BlockSpec((tm, w), lambda i: (i, 0))
    return pl.pallas_call(
        _ple_kernel,
        grid=(t // tm,),
        in_specs=[row(d), row(p2.shape[1]), _const_spec(g.shape), _const_spec(wg.shape), _const_spec(wp.shape)],
        out_specs=row(d),
        out_shape=jax.ShapeDtypeStruct((t, d), F32),
        compiler_params=_params("parallel"),
        name="ple",
    )(h2, p2, g, wg, wp)


def _layer(h2, p2, pos2, batch, seq, mix_norm_gain, w_in, w_pool, pool_scale, q_lat_gain, kv_lat_gain, w_uq, w_ukv,
           q_norm_gain, k_norm_gain, w_out, ffn_norm_gain, w_pq, sub_k1, sub_k2, expert_u, expert_v, ple_norm_gain,
           w_ple_gate, w_ple_proj):
    t, d = h2.shape
    q_lora = q_lat_gain.shape[0]
    kv_lora = kv_lat_gain.shape[0]
    n_heads = w_uq.shape[1] // QK_HEAD
    pool_w = w_in.shape[1] - q_lora - kv_lora - QK_ROPE
    rowv = lambda a: a.reshape(1, -1).astype(F32)

    win = jnp.pad(w_in, ((0, 0), (0, LANES - QK_ROPE))).astype(BF16)
    wuq = jnp.pad(w_uq.reshape(q_lora, n_heads, QK_HEAD), ((0, 0), (0, 0), (0, HEAD_SLOT - QK_HEAD)))
    wuq = wuq.reshape(q_lora, n_heads * HEAD_SLOT).astype(BF16)
    wukv = w_ukv.reshape(kv_lora, n_heads, 2, QK_NOPE).transpose(0, 2, 1, 3).reshape(kv_lora, -1).astype(BF16)
    pad_gain = lambda g: jnp.pad(g, (0, HEAD_SLOT - QK_HEAD)).reshape(1, HEAD_SLOT).astype(F32)
    gqh = pad_gain(q_norm_gain) * (QK_HEAD ** -0.5)
    gkh = pad_gain(k_norm_gain)
    inv_freq = ROPE_THETA ** (-jnp.arange(0, QK_ROPE, 2, dtype=F32) / QK_ROPE)
    invf = jnp.concatenate([inv_freq, inv_freq, jnp.zeros((LANES - QK_ROPE,), F32)]).reshape(1, LANES)

    zp, q, k, v = _mix_in(h2, pos2, rowv(mix_norm_gain), win, rowv(q_lat_gain), rowv(kv_lat_gain), wuq, wukv, gqh,
                          gkh, invf, pool_w=pool_w, q_lora=q_lora, kv_lora=kv_lora, n_heads=n_heads, tm=256)
    shp = lambda a: a.reshape(batch, seq, a.shape[1])
    o = _attention(shp(q), shp(k), shp(v), n_heads=n_heads, tq=256)
    h1 = _mix_out(zp, o.reshape(t, -1), h2, w_pool.astype(BF16), rowv(pool_scale), w_out.astype(BF16), seq=seq,
                  tm=256)
    xn, idx, gates = _route(h1, rowv(ffn_norm_gain), w_pq.astype(BF16), sub_k1.astype(BF16), sub_k2.astype(BF16),
                            tm=256)
    h3 = _experts(idx, xn, gates, h1, _pack_expert_table(expert_u, expert_v), tm=128)
    return _ple(h3, p2, rowv(ple_norm_gain), w_ple_gate.astype(BF16), w_ple_proj.astype(BF16), tm=256)


def kernel(x, p, positions, mix_norm_gain, w_in, w_pool, pool_scale, q_lat_gain, kv_lat_gain, w_uq, w_ukv,
           q_norm_gain, k_norm_gain, w_out, ffn_norm_gain, w_pq, sub_k1, sub_k2, expert_u, expert_v, ple_norm_gain,
           w_ple_gate, w_ple_proj):
    batch, seq, d = x.shape
    t = batch * seq
    h = x.reshape(t, d)
    pos2 = positions.reshape(t, 1).astype(jnp.int32)
    for i in range(w_in.shape[0]):
        h = _layer(h, p[i].reshape(t, -1), pos2, batch, seq, mix_norm_gain[i], w_in[i], w_pool[i], pool_scale[i],
                   q_lat_gain[i], kv_lat_gain[i], w_uq[i], w_ukv[i], q_norm_gain[i], k_norm_gain[i], w_out[i],
                   ffn_norm_gain[i], w_pq[i], sub_k1[i], sub_k2[i], expert_u[i], expert_v[i], ple_norm_gain[i],
                   w_ple_gate[i], w_ple_proj[i])
    return h.reshape(batch, seq, d)
```

```python
import functools

import jax
import jax.numpy as jnp
from jax import lax
from jax.experimental import pallas as pl
from jax.experimental.pallas import tpu as pltpu

F32 = jnp.float32
BF16 = jnp.bfloat16

EPS = 1e-6
CHUNK = 64
POOL_WINDOWS = (2, 4, 8, 16)
POOL_HALO = 16
V_HEAD = 128
QK_NOPE = 128
QK_ROPE = 64
QK_HEAD = QK_NOPE + QK_ROPE
HEAD_SLOT = 256
ROPE_THETA = 10000.0
PEER_HEADS = 8
PEER_NKEYS = 128
PEER_HALF = 128
PEER_TOPK = 16
PEER_SLOTS = PEER_HEADS * PEER_TOPK

LANES = 128
VMEM_LIMIT_BYTES = 56 * 1024 * 1024

OCTET = 8
GATHER_GROUP = 4
GROUPS_PER_OCTET = OCTET // GATHER_GROUP


def _rms(x):
    return x * lax.rsqrt(jnp.mean(x * x, axis=-1, keepdims=True) + EPS)


def _dot(a, b):
    return jnp.dot(a, b, preferred_element_type=F32)


def _dot_nt(a, b):
    return lax.dot_general(a, b, (((1,), (1,)), ((), ())), preferred_element_type=F32)


def _params(*sem):
    return pltpu.CompilerParams(dimension_semantics=sem, vmem_limit_bytes=VMEM_LIMIT_BYTES)


def _const_spec(shape):
    nd = len(shape)
    return pl.BlockSpec(shape, lambda *_: (0,) * nd)


def _mix_in_kernel(x_ref, pos_ref, g_ref, win_ref, gq_ref, gkv_ref, wuq_ref, wukv_ref, gqh_ref, gkh_ref,
                   invf_ref, zp_ref, q_ref, k_ref, v_ref, *, pool_w, q_lora, kv_lora, n_heads):
    xn = _rms(x_ref[...]) * g_ref[...]
    z = _dot(xn.astype(BF16), win_ref[...])
    zp_ref[...] = z[:, :pool_w]
    c1 = pool_w + q_lora
    c2 = c1 + kv_lora
    cq = _rms(z[:, pool_w:c1]) * gq_ref[...]
    ckv = _rms(z[:, c1:c2]) * gkv_ref[...]
    kr = z[:, c2:c2 + LANES]
    qx = _dot(cq.astype(BF16), wuq_ref[...])
    kvx = _dot(ckv.astype(BF16), wukv_ref[...])

    ang = pos_ref[...].astype(F32) * invf_ref[...]
    cos = jnp.cos(ang)
    sin = jnp.sin(ang)
    lane = lax.broadcasted_iota(jnp.int32, ang.shape, 1)
    first_half = lane < QK_ROPE // 2
    sin_signed = jnp.where(first_half, -sin, sin)

    def rope(y):
        partner = jnp.where(first_half, pltpu.roll(y, LANES - QK_ROPE // 2, 1), pltpu.roll(y, QK_ROPE // 2, 1))
        return y * cos + partner * sin_signed

    gq = gqh_ref[...]
    gk = gkh_ref[...]
    inv_w = 1.0 / QK_HEAD
    kr_ss = jnp.sum(kr * kr, axis=-1, keepdims=True)
    kr_roped = rope(kr * gk[:, QK_NOPE:])
    for h in range(n_heads):
        lo = h * HEAD_SLOT
        slab = qx[:, lo:lo + HEAD_SLOT]
        r = lax.rsqrt(jnp.sum(slab * slab, axis=-1, keepdims=True) * inv_w + EPS)
        q_ref[:, lo:lo + QK_NOPE] = (slab[:, :QK_NOPE] * r * gq[:, :QK_NOPE]).astype(BF16)
        q_ref[:, lo + QK_NOPE:lo + HEAD_SLOT] = rope(slab[:, QK_NOPE:] * r * gq[:, QK_NOPE:]).astype(BF16)
        kn = kvx[:, h * QK_NOPE:(h + 1) * QK_NOPE]
        rk = lax.rsqrt((jnp.sum(kn * kn, axis=-1, keepdims=True) + kr_ss) * inv_w + EPS)
        k_ref[:, lo:lo + QK_NOPE] = (kn * rk * gk[:, :QK_NOPE]).astype(BF16)
        k_ref[:, lo + QK_NOPE:lo + HEAD_SLOT] = (kr_roped * rk).astype(BF16)
    v_ref[...] = kvx[:, n_heads * QK_NOPE:].astype(BF16)


def _mix_in(x2, pos2, g, win, gq, gkv, wuq, wukv, gqh, gkh, invf, *, pool_w, q_lora, kv_lora, n_heads, tm):
    t, d = x2.shape
    kern = functools.partial(_mix_in_kernel, pool_w=pool_w, q_lora=q_lora, kv_lora=kv_lora, n_heads=n_heads)
    row = lambda w: pl.BlockSpec((tm, w), lambda i: (i, 0))
    return pl.pallas_call(
        kern,
        grid=(t // tm,),
        in_specs=[row(d), row(1), _const_spec(g.shape), _const_spec(win.shape), _const_spec(gq.shape),
                  _const_spec(gkv.shape), _const_spec(wuq.shape), _const_spec(wukv.shape), _const_spec(gqh.shape),
                  _const_spec(gkh.shape), _const_spec(invf.shape)],
        out_specs=[row(pool_w), row(n_heads * HEAD_SLOT), row(n_heads * HEAD_SLOT), row(n_heads * V_HEAD)],
        out_shape=[jax.ShapeDtypeStruct((t, pool_w), F32),
                   jax.ShapeDtypeStruct((t, n_heads * HEAD_SLOT), BF16),
                   jax.ShapeDtypeStruct((t, n_heads * HEAD_SLOT), BF16),
                   jax.ShapeDtypeStruct((t, n_heads * V_HEAD), BF16)],
        compiler_params=_params("parallel"),
        name="mix_in",
    )(x2, pos2, g, win, gq, gkv, wuq, wukv, gqh, gkh, invf)


def _attn_kernel(q_ref, k_ref, v_ref, o_ref, *, tq, tk):
    qi = pl.program_id(2)
    n_sub = tq // tk
    qs = [q_ref[s * tk:(s + 1) * tk, :] for s in range(n_sub)]
    q_chunks = [(qi * tq + s * tk + lax.broadcasted_iota(jnp.int32, (tk, 1), 0)) // CHUNK for s in range(n_sub)]

    def body(j, carry):
        start = pl.multiple_of(j * tk, tk)
        kj = k_ref[pl.ds(start, tk), :]
        vj = v_ref[pl.ds(start, tk), :]
        k_chunk = (start + lax.broadcasted_iota(jnp.int32, (1, tk), 1)) // CHUNK
        out = []
        for s in range(n_sub):
            m, l, acc = carry[s]
            sc = jnp.where(k_chunk <= q_chunks[s], _dot_nt(qs[s], kj), -jnp.inf)
            m_new = jnp.maximum(m, jnp.max(sc, axis=-1, keepdims=True))
            alpha = jnp.exp(m - m_new)
            p = jnp.exp(sc - m_new)
            l = alpha * l + jnp.sum(p, axis=-1, keepdims=True)
            acc = alpha * acc + _dot(p.astype(BF16), vj)
            out.append((m_new, l, acc))
        return tuple(out)

    init = tuple((jnp.full((tk, 1), -jnp.inf, F32), jnp.zeros((tk, 1), F32), jnp.zeros((tk, V_HEAD), F32))
                 for _ in range(n_sub))
    res = lax.fori_loop(0, (qi + 1) * n_sub, body, init)
    for s in range(n_sub):
        _, l, acc = res[s]
        o_ref[s * tk:(s + 1) * tk, :] = (acc / l).astype(BF16)


def _attention(q3, k3, v3, *, n_heads, tq, tk):
    b, s, _ = q3.shape
    return pl.pallas_call(
        functools.partial(_attn_kernel, tq=tq, tk=tk),
        grid=(b, n_heads, s // tq),
        in_specs=[pl.BlockSpec((None, tq, HEAD_SLOT), lambda bi, h, i: (bi, i, h)),
                  pl.BlockSpec((None, s, HEAD_SLOT), lambda bi, h, i: (bi, 0, h)),
                  pl.BlockSpec((None, s, V_HEAD), lambda bi, h, i: (bi, 0, h))],
        out_specs=pl.BlockSpec((None, tq, V_HEAD), lambda bi, h, i: (bi, i, h)),
        out_shape=jax.ShapeDtypeStruct((b, s, n_heads * V_HEAD), BF16),
        compiler_params=_params("parallel", "parallel", "arbitrary"),
        name="attn",
    )(q3, k3, v3)


def _mix_out_kernel(zp_ref, halo_ref, o_ref, x_ref, wpool_ref, pscale_ref, wout_ref, h_ref, ext_ref, *, seq, tm):
    t0 = (pl.program_id(0) * tm) % seq
    pool_w = zp_ref.shape[1]
    gw = pool_w // len(POOL_WINDOWS)
    ext_ref[0:POOL_HALO, :] = jnp.where(t0 == 0, 0.0, halo_ref[...])
    ext_ref[POOL_HALO:, :] = zp_ref[...]
    tpos = t0 + lax.broadcasted_iota(jnp.int32, (tm, 1), 0)
    acc = x_ref[...] + _dot(o_ref[...], wout_ref[pool_w:, :])
    for g, w in enumerate(POOL_WINDOWS):
        lo = g * gw
        cur = ext_ref[POOL_HALO:POOL_HALO + tm, lo:lo + gw]
        win_sum = cur
        for j in range(1, w):
            win_sum = win_sum + ext_ref[POOL_HALO - j:POOL_HALO - j + tm, lo:lo + gw]
        cnt = jnp.minimum(tpos + 1, w).astype(F32)
        dlt = win_sum / cnt - cur
        y = _dot(dlt.astype(BF16), wpool_ref[g]) * pscale_ref[:, lo:lo + gw]
        acc = acc + _dot(y.astype(BF16), wout_ref[lo:lo + gw, :])
    h_ref[...] = acc


def _mix_out(zp, o2, x2, wpool, pscale, wout, *, seq, tm):
    t, d = x2.shape
    pool_w = zp.shape[1]
    hb = tm // POOL_HALO
    row = lambda w: pl.BlockSpec((tm, w), lambda i: (i, 0))
    return pl.pallas_call(
        functools.partial(_mix_out_kernel, seq=seq, tm=tm),
        grid=(t // tm,),
        in_specs=[row(pool_w),
                  pl.BlockSpec((POOL_HALO, pool_w), lambda i: (jnp.maximum(i * hb - 1, 0), 0)),
                  row(o2.shape[1]), row(d), _const_spec(wpool.shape), _const_spec(pscale.shape),
                  _const_spec(wout.shape)],
        out_specs=row(d),
        out_shape=jax.ShapeDtypeStruct((t, d), F32),
        scratch_shapes=[pltpu.VMEM((tm + POOL_HALO, pool_w), F32)],
        compiler_params=_params("parallel"),
        name="mix_out",
    )(zp, zp, o2, x2, wpool, pscale, wout)


def _topk_cols(s, n):
    tm = s.shape[1]
    rows = lax.broadcasted_iota(jnp.int32, s.shape, 0).astype(F32)
    slot = lax.broadcasted_iota(jnp.int32, (PEER_TOPK, tm), 0)
    vals = jnp.zeros((PEER_TOPK, tm), F32)
    idxs = jnp.zeros((PEER_TOPK, tm), F32)
    for k in range(PEER_TOPK):
        m = jnp.max(s, axis=0, keepdims=True)
        i = jnp.min(jnp.where(s == m, rows, float(n)), axis=0, keepdims=True)
        vals = jnp.where(slot == k, m, vals)
        idxs = jnp.where(slot == k, i, idxs)
        s = jnp.where(rows == i, -jnp.inf, s)
    return vals, idxs


def _route_kernel(h_ref, g_ref, wpq_ref, k1_ref, k2_ref, xn_ref, idx_ref, gate_ref, qp_ref, idxt_ref, gatet_ref):
    xn = _rms(h_ref[...]) * g_ref[...]
    xn_ref[...] = xn
    qp_ref[...] = _dot(xn.astype(BF16), wpq_ref[...]).astype(BF16)

    def head(hd, carry):
        off = pl.multiple_of(hd * 2 * PEER_HALF, 2 * PEER_HALF)
        s1 = _dot_nt(k1_ref[hd], qp_ref[:, pl.ds(off, PEER_HALF)])
        s2 = _dot_nt(k2_ref[hd], qp_ref[:, pl.ds(off + PEER_HALF, PEER_HALF)])
        v1, i1 = _topk_cols(s1, PEER_NKEYS)
        v2, i2 = _topk_cols(s2, PEER_NKEYS)
        cand = jnp.concatenate([v1[a:a + 1, :] + v2 for a in range(PEER_TOPK)], axis=0)
        vs, ci = _topk_cols(cand, PEER_TOPK * PEER_TOPK)
        ci = ci.astype(jnp.int32)
        a_sel = ci // PEER_TOPK
        b_sel = ci % PEER_TOPK
        e1 = jnp.zeros_like(vs)
        e2 = jnp.zeros_like(vs)
        for a in range(PEER_TOPK):
            e1 = jnp.where(a_sel == a, i1[a:a + 1, :], e1)
            e2 = jnp.where(b_sel == a, i2[a:a + 1, :], e2)
        ex = jnp.exp(vs - jnp.max(vs, axis=0, keepdims=True))
        r0 = pl.multiple_of(hd * PEER_TOPK, PEER_TOPK)
        idxt_ref[pl.ds(r0, PEER_TOPK), :] = (e1 * PEER_NKEYS + e2).astype(jnp.int32)
        gatet_ref[pl.ds(r0, PEER_TOPK), :] = ex / jnp.sum(ex, axis=0, keepdims=True)
        return carry

    lax.fori_loop(0, PEER_HEADS, head, 0)
    idx_ref[...] = idxt_ref[...].T
    gate_ref[...] = gatet_ref[...].T


def _route(h1, g, wpq, k1, k2, *, tm):
    t, d = h1.shape
    row = lambda w: pl.BlockSpec((tm, w), lambda i: (i, 0))
    return pl.pallas_call(
        _route_kernel,
        grid=(t // tm,),
        in_specs=[row(d), _const_spec(g.shape), _const_spec(wpq.shape), _const_spec(k1.shape),
                  _const_spec(k2.shape)],
        out_specs=[row(d), row(PEER_SLOTS), row(PEER_SLOTS)],
        out_shape=[jax.ShapeDtypeStruct((t, d), F32),
                   jax.ShapeDtypeStruct((t, PEER_SLOTS), jnp.int32),
                   jax.ShapeDtypeStruct((t, PEER_SLOTS), F32)],
        scratch_shapes=[pltpu.VMEM((tm, wpq.shape[1]), BF16),
                        pltpu.VMEM((PEER_SLOTS, tm), jnp.int32),
                        pltpu.VMEM((PEER_SLOTS, tm), F32)],
        compiler_params=_params("parallel"),
        name="peer_route",
    )(h1, g, wpq, k1, k2)


def _experts_kernel(idx_hbm, xn_ref, gate_ref, h_ref, expand_ref, tab_hbm, out_ref, idx_smem, b00, b01, b10, b11,
                    cols_ref, sem, idx_sem, *, tm, n_tiles):
    bufs = ((b00, b01), (b10, b11))
    n_chunk = b00.shape[1]
    n_oct = tm // OCTET
    i = pl.program_id(0)
    idx_cp = pltpu.make_async_copy(idx_hbm.at[pl.ds(i * tm, tm + OCTET), :], idx_smem, idx_sem.at[0])
    idx_cp.start()
    idx_cp.wait()

    def issue(o, gg, half):
        for tt in range(GATHER_GROUP):
            t = o * OCTET + gg * GATHER_GROUP + tt
            for k in range(PEER_SLOTS):
                pltpu.make_async_copy(tab_hbm.at[idx_smem[t, k]], bufs[half][gg].at[tt, :, pl.ds(2 * k, 2), :],
                                      sem.at[half, gg]).start(priority=k % 2)

    def wait(gg, half):
        pltpu.make_async_copy(bufs[half][gg], bufs[half][gg], sem.at[half, gg]).wait()

    lane = lax.broadcasted_iota(jnp.int32, (2 * PEER_SLOTS, LANES), 1)
    sub = lax.broadcasted_iota(jnp.int32, (OCTET, 2 * PEER_SLOTS), 0)

    def scores(o, gg, half):
        buf = bufs[half][gg]
        base = pl.multiple_of(o * OCTET, OCTET)
        x8 = xn_ref[pl.ds(base, OCTET), :]
        cols = jnp.zeros((2 * PEER_SLOTS, LANES), F32)
        for tt in range(GATHER_GROUP):
            j = gg * GATHER_GROUP + tt
            part = jnp.zeros((2 * PEER_SLOTS, LANES), F32)
            for c in range(n_chunk):
                part = part + buf[tt, c].astype(F32) * x8[j:j + 1, c * LANES:(c + 1) * LANES]
            cols = jnp.where(lane == j, jnp.sum(part, axis=-1, keepdims=True), cols)
        cols_ref[...] = cols
        act = cols_ref[pl.ds(0, PEER_SLOTS, stride=2), :].T[:OCTET, :]
        wgt = jax.nn.gelu(act) * gate_ref[pl.ds(base, OCTET), :]
        return _dot(wgt.astype(BF16), expand_ref[...])

    def combine(o, gg, half, wgt):
        buf = bufs[half][gg]
        base = pl.multiple_of(o * OCTET, OCTET)
        total = (h_ref if gg == 0 else out_ref)[pl.ds(base, OCTET), :]
        for tt in range(GATHER_GROUP):
            j = gg * GATHER_GROUP + tt
            wj = jnp.where(sub == j, wgt, 0.0).astype(BF16)
            total = total + jnp.concatenate([_dot(wj, buf[tt, c]) for c in range(n_chunk)], axis=1)
        out_ref[pl.ds(base, OCTET), :] = total

    def octet(o, half, w_prev):
        wait(0, half)
        issue(o + 1, 0, 1 - half)
        if w_prev is not None:
            combine(o - 1, 1, 1 - half, w_prev)
        w0 = scores(o, 0, half)
        wait(1, half)
        issue(o + 1, 1, 1 - half)
        combine(o, 0, half, w0)
        return scores(o, 1, half)

    @pl.when(i == 0)
    def _():
        for gg in range(GROUPS_PER_OCTET):
            issue(0, gg, 0)

    def octet_pair(pi, w):
        w = octet(2 * pi + 1, 1, w)
        return octet(2 * pi + 2, 0, w)

    w = octet(0, 0, None)
    w = lax.fori_loop(0, (n_oct - 2) // 2, octet_pair, w)
    w = octet(n_oct - 1, 1, w)
    combine(n_oct - 1, 1, 1, w)

    @pl.when(i == n_tiles - 1)
    def _():
        for gg in range(GROUPS_PER_OCTET):
            wait(gg, 0)


def _experts(idx, xn, gates, h1, table, *, tm):
    t, d = xn.shape
    n_chunk = d // LANES
    row = lambda w: pl.BlockSpec((tm, w), lambda i: (i, 0))
    slot_ids = jnp.arange(PEER_SLOTS)
    expand = jnp.zeros((PEER_SLOTS, 2 * PEER_SLOTS), BF16).at[slot_ids, 2 * slot_ids + 1].set(1)
    idx_pad = jnp.pad(idx, ((0, OCTET), (0, 0)))
    gather_buf = pltpu.VMEM((GATHER_GROUP, n_chunk, 2 * PEER_SLOTS, LANES), BF16)
    return pl.pallas_call(
        functools.partial(_experts_kernel, tm=tm, n_tiles=t // tm),
        grid=(t // tm,),
        in_specs=[pl.BlockSpec(memory_space=pl.ANY), row(d), row(PEER_SLOTS), row(d), _const_spec(expand.shape),
                  pl.BlockSpec(memory_space=pl.ANY)],
        out_specs=row(d),
        out_shape=jax.ShapeDtypeStruct((t, d), F32),
        scratch_shapes=[pltpu.SMEM((tm + OCTET, PEER_SLOTS), jnp.int32),
                        gather_buf, gather_buf, gather_buf, gather_buf,
                        pltpu.VMEM((2 * PEER_SLOTS, LANES), F32),
                        pltpu.SemaphoreType.DMA((2, GROUPS_PER_OCTET)),
                        pltpu.SemaphoreType.DMA((1,))],
        compiler_params=_params("arbitrary"),
        name="peer_experts",
    )(idx_pad, xn, gates, h1, expand, table)


def _pair_expert_table(expert_u, expert_v):
    e, d = expert_u.shape
    u = expert_u.reshape(e, d // LANES, 1, LANES)
    v = expert_v.reshape(e, d // LANES, 1, LANES)
    return jnp.concatenate([u, v], axis=2).astype(BF16)


def _ple_kernel(h_ref, p_ref, g_ref, wg_ref, wp_ref, o_ref):
    h = h_ref[...]
    xn = _rms(h) * g_ref[...]
    gate = jax.nn.sigmoid(_dot(xn.astype(BF16), wg_ref[...]))
    o_ref[...] = h + gate * _dot(p_ref[...].astype(BF16), wp_ref[...])


def _ple(h2, p2, g, wg, wp, *, tm):
    t, d = h2.shape
    row = lambda w: pl.BlockSpec((tm, w), lambda i: (i, 0))
    return pl.pallas_call(
        _ple_kernel,
        grid=(t // tm,),
        in_specs=[row(d), row(p2.shape[1]), _const_spec(g.shape), _const_spec(wg.shape), _const_spec(wp.shape)],
        out_specs=row(d),
        out_shape=jax.ShapeDtypeStruct((t, d), F32),
        compiler_params=_params("parallel"),
        name="ple",
    )(h2, p2, g, wg, wp)


def _layer(h2, p2, pos2, batch, seq, mix_norm_gain, w_in, w_pool, pool_scale, q_lat_gain, kv_lat_gain, w_uq, w_ukv,
           q_norm_gain, k_norm_gain, w_out, ffn_norm_gain, w_pq, sub_k1, sub_k2, expert_u, expert_v, ple_norm_gain,
           w_ple_gate, w_ple_proj):
    t, d = h2.shape
    q_lora = q_lat_gain.shape[0]
    kv_lora = kv_lat_gain.shape[0]
    n_heads = w_uq.shape[1] // QK_HEAD
    pool_w = w_in.shape[1] - q_lora - kv_lora - QK_ROPE
    rowv = lambda a: a.reshape(1, -1).astype(F32)

    win = jnp.pad(w_in, ((0, 0), (0, LANES - QK_ROPE))).astype(BF16)
    wuq = jnp.pad(w_uq.reshape(q_lora, n_heads, QK_HEAD), ((0, 0), (0, 0), (0, HEAD_SLOT - QK_HEAD)))
    wuq = wuq.reshape(q_lora, n_heads * HEAD_SLOT).astype(BF16)
    wukv = w_ukv.reshape(kv_lora, n_heads, 2, QK_NOPE).transpose(0, 2, 1, 3).reshape(kv_lora, -1).astype(BF16)
    pad_gain = lambda g: jnp.pad(g, (0, HEAD_SLOT - QK_HEAD)).reshape(1, HEAD_SLOT).astype(F32)
    gqh = pad_gain(q_norm_gain) * (QK_HEAD ** -0.5)
    gkh = pad_gain(k_norm_gain)
    inv_freq = ROPE_THETA ** (-jnp.arange(0, QK_ROPE, 2, dtype=F32) / QK_ROPE)
    invf = jnp.concatenate([inv_freq, inv_freq, jnp.zeros((LANES - QK_ROPE,), F32)]).reshape(1, LANES)

    zp, q, k, v = _mix_in(h2, pos2, rowv(mix_norm_gain), win, rowv(q_lat_gain), rowv(kv_lat_gain), wuq, wukv, gqh,
                          gkh, invf, pool_w=pool_w, q_lora=q_lora, kv_lora=kv_lora, n_heads=n_heads, tm=256)
    shp = lambda a: a.reshape(batch, seq, a.shape[1])
    o = _attention(shp(q), shp(k), shp(v), n_heads=n_heads, tq=512, tk=256)
    h1 = _mix_out(zp, o.reshape(t, -1), h2, w_pool.astype(BF16), rowv(pool_scale), w_out.astype(BF16), seq=seq,
                  tm=256)
    xn, idx, gates = _route(h1, rowv(ffn_norm_gain), w_pq.astype(BF16), sub_k1.astype(BF16), sub_k2.astype(BF16),
                            tm=256)
    h3 = _experts(idx, xn, gates, h1, _pair_expert_table(expert_u, expert_v), tm=128)
    return _ple(h3, p2, rowv(ple_norm_gain), w_ple_gate.astype(BF16), w_ple_proj.astype(BF16), tm=256)


def kernel(x, p, positions, mix_norm_gain, w_in, w_pool, pool_scale, q_lat_gain, kv_lat_gain, w_uq, w_ukv,
           q_norm_gain, k_norm_gain, w_out, ffn_norm_gain, w_pq, sub_k1, sub_k2, expert_u, expert_v, ple_norm_gain,
           w_ple_gate, w_ple_proj):
    batch, seq, d = x.shape
    t = batch * seq
    h = x.reshape(t, d)
    pos2 = positions.reshape(t, 1).astype(jnp.int32)
    for i in range(w_in.shape[0]):
        h = _layer(h, p[i].reshape(t, -1), pos2, batch, seq, mix_norm_gain[i], w_in[i], w_pool[i], pool_scale[i],
                   q_lat_gain[i], kv_lat_gain[i], w_uq[i], w_ukv[i], q_norm_gain[i], k_norm_gain[i], w_out[i],
                   ffn_norm_gain[i], w_pq[i], sub_k1[i], sub_k2[i], expert_u[i], expert_v[i], ple_norm_gain[i],
                   w_ple_gate[i], w_ple_proj[i])
    return h.reshape(batch, seq, d)
```

```python
import functools

import jax
import jax.numpy as jnp
from jax import lax
from jax.experimental import pallas as pl
from jax.experimental.pallas import tpu as pltpu

F32 = jnp.float32
BF16 = jnp.bfloat16

EPS = 1e-6
CHUNK = 64
POOL_WINDOWS = (2, 4, 8, 16)
POOL_HALO = 16
V_HEAD = 128
QK_NOPE = 128
QK_ROPE = 64
QK_HEAD = QK_NOPE + QK_ROPE
HEAD_SLOT = 256
ROPE_THETA = 10000.0
PEER_HEADS = 8
PEER_NKEYS = 128
PEER_HALF = 128
PEER_TOPK = 16
PEER_SLOTS = PEER_HEADS * PEER_TOPK

LANES = 128
VMEM_LIMIT_BYTES = 56 * 1024 * 1024

OCTET = 8
GATHER_GROUP = 4
GROUPS_PER_OCTET = OCTET // GATHER_GROUP


def _rms(x):
    return x * lax.rsqrt(jnp.mean(x * x, axis=-1, keepdims=True) + EPS)


def _dot(a, b):
    return jnp.dot(a, b, preferred_element_type=F32)


def _dot_nt(a, b):
    return lax.dot_general(a, b, (((1,), (1,)), ((), ())), preferred_element_type=F32)


def _params(*sem):
    return pltpu.CompilerParams(dimension_semantics=sem, vmem_limit_bytes=VMEM_LIMIT_BYTES)


def _const_spec(shape):
    nd = len(shape)
    return pl.BlockSpec(shape, lambda *_: (0,) * nd)


def _mix_in_kernel(x_ref, pos_ref, g_ref, win_ref, gq_ref, gkv_ref, wuq_ref, wukv_ref, gqh_ref, gkh_ref,
                   invf_ref, zp_ref, q_ref, k_ref, v_ref, *, pool_w, q_lora, kv_lora, n_heads):
    xn = _rms(x_ref[...]) * g_ref[...]
    z = _dot(xn.astype(BF16), win_ref[...])
    zp_ref[...] = z[:, :pool_w]
    c1 = pool_w + q_lora
    c2 = c1 + kv_lora
    cq = _rms(z[:, pool_w:c1]) * gq_ref[...]
    ckv = _rms(z[:, c1:c2]) * gkv_ref[...]
    kr = z[:, c2:c2 + LANES]
    qx = _dot(cq.astype(BF16), wuq_ref[...])
    kvx = _dot(ckv.astype(BF16), wukv_ref[...])

    ang = pos_ref[...].astype(F32) * invf_ref[...]
    cos = jnp.cos(ang)
    sin = jnp.sin(ang)
    lane = lax.broadcasted_iota(jnp.int32, ang.shape, 1)
    first_half = lane < QK_ROPE // 2
    sin_signed = jnp.where(first_half, -sin, sin)

    def rope(y):
        partner = jnp.where(first_half, pltpu.roll(y, LANES - QK_ROPE // 2, 1), pltpu.roll(y, QK_ROPE // 2, 1))
        return y * cos + partner * sin_signed

    gq = gqh_ref[...]
    gk = gkh_ref[...]
    inv_w = 1.0 / QK_HEAD
    kr_ss = jnp.sum(kr * kr, axis=-1, keepdims=True)
    kr_roped = rope(kr * gk[:, QK_NOPE:])
    for h in range(n_heads):
        lo = h * HEAD_SLOT
        slab = qx[:, lo:lo + HEAD_SLOT]
        r = lax.rsqrt(jnp.sum(slab * slab, axis=-1, keepdims=True) * inv_w + EPS)
        q_ref[:, lo:lo + QK_NOPE] = (slab[:, :QK_NOPE] * r * gq[:, :QK_NOPE]).astype(BF16)
        q_ref[:, lo + QK_NOPE:lo + HEAD_SLOT] = rope(slab[:, QK_NOPE:] * r * gq[:, QK_NOPE:]).astype(BF16)
        kn = kvx[:, h * QK_NOPE:(h + 1) * QK_NOPE]
        rk = lax.rsqrt((jnp.sum(kn * kn, axis=-1, keepdims=True) + kr_ss) * inv_w + EPS)
        k_ref[:, lo:lo + QK_NOPE] = (kn * rk * gk[:, :QK_NOPE]).astype(BF16)
        k_ref[:, lo + QK_NOPE:lo + HEAD_SLOT] = (kr_roped * rk).astype(BF16)
    v_ref[...] = kvx[:, n_heads * QK_NOPE:].astype(BF16)


def _mix_in(x2, pos2, g, win, gq, gkv, wuq, wukv, gqh, gkh, invf, *, pool_w, q_lora, kv_lora, n_heads, tm):
    t, d = x2.shape
    kern = functools.partial(_mix_in_kernel, pool_w=pool_w, q_lora=q_lora, kv_lora=kv_lora, n_heads=n_heads)
    row = lambda w: pl.BlockSpec((tm, w), lambda i: (i, 0))
    return pl.pallas_call(
        kern,
        grid=(t // tm,),
        in_specs=[row(d), row(1), _const_spec(g.shape), _const_spec(win.shape), _const_spec(gq.shape),
                  _const_spec(gkv.shape), _const_spec(wuq.shape), _const_spec(wukv.shape), _const_spec(gqh.shape),
                  _const_spec(gkh.shape), _const_spec(invf.shape)],
        out_specs=[row(pool_w), row(n_heads * HEAD_SLOT), row(n_heads * HEAD_SLOT), row(n_heads * V_HEAD)],
        out_shape=[jax.ShapeDtypeStruct((t, pool_w), F32),
                   jax.ShapeDtypeStruct((t, n_heads * HEAD_SLOT), BF16),
                   jax.ShapeDtypeStruct((t, n_heads * HEAD_SLOT), BF16),
                   jax.ShapeDtypeStruct((t, n_heads * V_HEAD), BF16)],
        compiler_params=_params("parallel"),
        name="mix_in",
    )(x2, pos2, g, win, gq, gkv, wuq, wukv, gqh, gkh, invf)


def _attn_kernel(q_ref, k_ref, v_ref, o_ref, *, tq, tk):
    qi = pl.program_id(2)
    n_sub = tq // tk
    qs = [q_ref[s * tk:(s + 1) * tk, :] for s in range(n_sub)]
    q_chunks = [(qi * tq + s * tk + lax.broadcasted_iota(jnp.int32, (tk, 1), 0)) // CHUNK for s in range(n_sub)]

    def body(j, carry):
        start = pl.multiple_of(j * tk, tk)
        kj = k_ref[pl.ds(start, tk), :]
        vj = v_ref[pl.ds(start, tk), :]
        k_chunk = (start + lax.broadcasted_iota(jnp.int32, (1, tk), 1)) // CHUNK
        out = []
        for s in range(n_sub):
            m, l, acc = carry[s]
            sc = jnp.where(k_chunk <= q_chunks[s], _dot_nt(qs[s], kj), -jnp.inf)
            m_new = jnp.maximum(m, jnp.max(sc, axis=-1, keepdims=True))
            alpha = jnp.exp(m - m_new)
            p = jnp.exp(sc - m_new)
            l = alpha * l + jnp.sum(p, axis=-1, keepdims=True)
            acc = alpha * acc + _dot(p.astype(BF16), vj)
            out.append((m_new, l, acc))
        return tuple(out)

    init = tuple((jnp.full((tk, 1), -jnp.inf, F32), jnp.zeros((tk, 1), F32), jnp.zeros((tk, V_HEAD), F32))
                 for _ in range(n_sub))
    res = lax.fori_loop(0, (qi + 1) * n_sub, body, init)
    for s in range(n_sub):
        _, l, acc = res[s]
        o_ref[s * tk:(s + 1) * tk, :] = (acc / l).astype(BF16)


def _attention(q3, k3, v3, *, n_heads, tq, tk):
    b, s, _ = q3.shape
    return pl.pallas_call(
        functools.partial(_attn_kernel, tq=tq, tk=tk),
        grid=(b, n_heads, s // tq),
        in_specs=[pl.BlockSpec((None, tq, HEAD_SLOT), lambda bi, h, i: (bi, i, h)),
                  pl.BlockSpec((None, s, HEAD_SLOT), lambda bi, h, i: (bi, 0, h)),
                  pl.BlockSpec((None, s, V_HEAD), lambda bi, h, i: (bi, 0, h))],
        out_specs=pl.BlockSpec((None, tq, V_HEAD), lambda bi, h, i: (bi, i, h)),
        out_shape=jax.ShapeDtypeStruct((b, s, n_heads * V_HEAD), BF16),
        compiler_params=_params("parallel", "parallel", "arbitrary"),
        name="attn",
    )(q3, k3, v3)


def _mix_out_kernel(zp_ref, halo_ref, o_ref, x_ref, wpool_ref, pscale_ref, wout_ref, h_ref, ext_ref, *, seq, tm):
    t0 = (pl.program_id(0) * tm) % seq
    pool_w = zp_ref.shape[1]
    gw = pool_w // len(POOL_WINDOWS)
    ext_ref[0:POOL_HALO, :] = jnp.where(t0 == 0, 0.0, halo_ref[...])
    ext_ref[POOL_HALO:, :] = zp_ref[...]
    tpos = t0 + lax.broadcasted_iota(jnp.int32, (tm, 1), 0)
    acc = x_ref[...] + _dot(o_ref[...], wout_ref[pool_w:, :])
    for g, w in enumerate(POOL_WINDOWS):
        lo = g * gw
        cur = ext_ref[POOL_HALO:POOL_HALO + tm, lo:lo + gw]
        win_sum = cur
        for j in range(1, w):
            win_sum = win_sum + ext_ref[POOL_HALO - j:POOL_HALO - j + tm, lo:lo + gw]
        cnt = jnp.minimum(tpos + 1, w).astype(F32)
        dlt = win_sum / cnt - cur
        y = _dot(dlt.astype(BF16), wpool_ref[g]) * pscale_ref[:, lo:lo + gw]
        acc = acc + _dot(y.astype(BF16), wout_ref[lo:lo + gw, :])
    h_ref[...] = acc


def _mix_out(zp, o2, x2, wpool, pscale, wout, *, seq, tm):
    t, d = x2.shape
    pool_w = zp.shape[1]
    hb = tm // POOL_HALO
    row = lambda w: pl.BlockSpec((tm, w), lambda i: (i, 0))
    return pl.pallas_call(
        functools.partial(_mix_out_kernel, seq=seq, tm=tm),
        grid=(t // tm,),
        in_specs=[row(pool_w),
                  pl.BlockSpec((POOL_HALO, pool_w), lambda i: (jnp.maximum(i * hb - 1, 0), 0)),
                  row(o2.shape[1]), row(d), _const_spec(wpool.shape), _const_spec(pscale.shape),
                  _const_spec(wout.shape)],
        out_specs=row(d),
        out_shape=jax.ShapeDtypeStruct((t, d), F32),
        scratch_shapes=[pltpu.VMEM((tm + POOL_HALO, pool_w), F32)],
        compiler_params=_params("parallel"),
        name="mix_out",
    )(zp, zp, o2, x2, wpool, pscale, wout)


def _topk_cols(s, n, rows=None):
    tm = s.shape[1]
    if rows is None:
        rows = lax.broadcasted_iota(jnp.int32, s.shape, 0).astype(F32)
    slot = lax.broadcasted_iota(jnp.int32, (PEER_TOPK, tm), 0)
    vals = jnp.zeros((PEER_TOPK, tm), F32)
    idxs = jnp.zeros((PEER_TOPK, tm), F32)
    for k in range(PEER_TOPK):
        m = jnp.max(s, axis=0, keepdims=True)
        i = jnp.min(jnp.where(s == m, rows, float(n)), axis=0, keepdims=True)
        vals = jnp.where(slot == k, m, vals)
        idxs = jnp.where(slot == k, i, idxs)
        s = jnp.where(rows == i, -jnp.inf, s)
    return vals, idxs


def _route_kernel(h_ref, g_ref, wpq_ref, k1_ref, k2_ref, xn_ref, idx_ref, gate_ref, qp_ref, idxt_ref, gatet_ref):
    xn = _rms(h_ref[...]) * g_ref[...]
    xn_ref[...] = xn
    qp_ref[...] = _dot(xn.astype(BF16), wpq_ref[...]).astype(BF16)

    def head(hd, carry):
        off = pl.multiple_of(hd * 2 * PEER_HALF, 2 * PEER_HALF)
        s1 = _dot_nt(k1_ref[hd], qp_ref[:, pl.ds(off, PEER_HALF)])
        s2 = _dot_nt(k2_ref[hd], qp_ref[:, pl.ds(off + PEER_HALF, PEER_HALF)])
        v1, i1 = _topk_cols(s1, PEER_NKEYS)
        v2, i2 = _topk_cols(s2, PEER_NKEYS)
        half_k = PEER_TOPK // 2
        b_ids = lax.broadcasted_iota(jnp.int32, (half_k, 1), 0)
        pieces = [v1[0:1, :] + v2]
        ids = [lax.broadcasted_iota(jnp.int32, (PEER_TOPK, 1), 0)]
        for a in range(1, PEER_TOPK):
            pieces.append(jnp.where(b_ids < PEER_TOPK // (a + 1), v1[a:a + 1, :] + v2[:half_k, :], -jnp.inf))
            ids.append(a * PEER_TOPK + b_ids)
        cand = jnp.concatenate(pieces, axis=0)
        vs, ci = _topk_cols(cand, PEER_TOPK * PEER_TOPK, jnp.concatenate(ids, axis=0).astype(F32))
        ci = ci.astype(jnp.int32)
        a_sel = ci // PEER_TOPK
        b_sel = ci % PEER_TOPK
        e1 = jnp.zeros_like(vs)
        e2 = jnp.zeros_like(vs)
        for a in range(PEER_TOPK):
            e1 = jnp.where(a_sel == a, i1[a:a + 1, :], e1)
            e2 = jnp.where(b_sel == a, i2[a:a + 1, :], e2)
        ex = jnp.exp(vs - jnp.max(vs, axis=0, keepdims=True))
        r0 = pl.multiple_of(hd * PEER_TOPK, PEER_TOPK)
        idxt_ref[pl.ds(r0, PEER_TOPK), :] = (e1 * PEER_NKEYS + e2).astype(jnp.int32)
        gatet_ref[pl.ds(r0, PEER_TOPK), :] = ex / jnp.sum(ex, axis=0, keepdims=True)
        return carry

    lax.fori_loop(0, PEER_HEADS, head, 0)
    idx_ref[...] = idxt_ref[...].T
    gate_ref[...] = gatet_ref[...].T


def _route(h1, g, wpq, k1, k2, *, tm):
    t, d = h1.shape
    row = lambda w: pl.BlockSpec((tm, w), lambda i: (i, 0))
    return pl.pallas_call(
        _route_kernel,
        grid=(t // tm,),
        in_specs=[row(d), _const_spec(g.shape), _const_spec(wpq.shape), _const_spec(k1.shape),
                  _const_spec(k2.shape)],
        out_specs=[row(d), row(PEER_SLOTS), row(PEER_SLOTS)],
        out_shape=[jax.ShapeDtypeStruct((t, d), F32),
                   jax.ShapeDtypeStruct((t, PEER_SLOTS), jnp.int32),
                   jax.ShapeDtypeStruct((t, PEER_SLOTS), F32)],
        scratch_shapes=[pltpu.VMEM((tm, wpq.shape[1]), BF16),
                        pltpu.VMEM((PEER_SLOTS, tm), jnp.int32),
                        pltpu.VMEM((PEER_SLOTS, tm), F32)],
        compiler_params=_params("parallel"),
        name="peer_route",
    )(h1, g, wpq, k1, k2)


def _experts_kernel(idx_hbm, xn_ref, gate_ref, h_ref, expand_ref, tab_hbm, out_ref, idx_smem, b00, b01, b10, b11,
                    cols_ref, sem, idx_sem, *, tm, n_tiles):
    bufs = ((b00, b01), (b10, b11))
    n_chunk = b00.shape[1]
    n_oct = tm // OCTET
    i = pl.program_id(0)
    idx_cp = pltpu.make_async_copy(idx_hbm.at[pl.ds(i * tm, tm + OCTET), :], idx_smem, idx_sem.at[0])
    idx_cp.start()
    idx_cp.wait()

    def issue(o, gg, tt, half):
        t = o * OCTET + gg * GATHER_GROUP + tt
        for k in range(PEER_SLOTS):
            pltpu.make_async_copy(tab_hbm.at[idx_smem[t, k]], bufs[half][gg].at[tt, :, pl.ds(2 * k, 2), :],
                                  sem.at[half, gg, tt]).start(priority=k % 2)

    def wait(gg, tt, half):
        pltpu.make_async_copy(bufs[half][gg].at[tt], bufs[half][gg].at[tt], sem.at[half, gg, tt]).wait()

    lane = lax.broadcasted_iota(jnp.int32, (2 * PEER_SLOTS, LANES), 1)
    sub = lax.broadcasted_iota(jnp.int32, (OCTET, 2 * PEER_SLOTS), 0)

    def combine_token(o, gg, tt, half, wgt, total):
        j = gg * GATHER_GROUP + tt
        wj = jnp.where(sub == j, wgt, 0.0).astype(BF16)
        return total + jnp.concatenate([_dot(wj, bufs[half][gg][tt, c]) for c in range(n_chunk)], axis=1)

    def group(o, gg, half, prev):
        buf = bufs[half][gg]
        base = pl.multiple_of(o * OCTET, OCTET)
        x8 = xn_ref[pl.ds(base, OCTET), :]
        cols = jnp.zeros((2 * PEER_SLOTS, LANES), F32)
        if prev is not None:
            po, pg, ph, pw = prev
            pbase = pl.multiple_of(po * OCTET, OCTET)
            total = (h_ref if pg == 0 else out_ref)[pl.ds(pbase, OCTET), :]
        for tt in range(GATHER_GROUP):
            wait(gg, tt, half)
            issue(o + 1, gg, tt, 1 - half)
            j = gg * GATHER_GROUP + tt
            part = jnp.zeros((2 * PEER_SLOTS, LANES), F32)
            for c in range(n_chunk):
                part = part + buf[tt, c].astype(F32) * x8[j:j + 1, c * LANES:(c + 1) * LANES]
            cols = jnp.where(lane == j, jnp.sum(part, axis=-1, keepdims=True), cols)
            if prev is not None:
                total = combine_token(po, pg, tt, ph, pw, total)
        if prev is not None:
            out_ref[pl.ds(pbase, OCTET), :] = total
        cols_ref[...] = cols
        act = cols_ref[pl.ds(0, PEER_SLOTS, stride=2), :].T[:OCTET, :]
        wgt = jax.nn.gelu(act) * gate_ref[pl.ds(base, OCTET), :]
        return _dot(wgt.astype(BF16), expand_ref[...])

    def octet(o, half, w_prev):
        w0 = group(o, 0, half, None if w_prev is None else (o - 1, 1, 1 - half, w_prev))
        return group(o, 1, half, (o, 0, half, w0))

    @pl.when(i == 0)
    def _():
        for gg in range(GROUPS_PER_OCTET):
            for tt in range(GATHER_GROUP):
                issue(0, gg, tt, 0)

    def octet_pair(pi, w):
        w = octet(2 * pi + 1, 1, w)
        return octet(2 * pi + 2, 0, w)

    w = octet(0, 0, None)
    w = lax.fori_loop(0, (n_oct - 2) // 2, octet_pair, w)
    w = octet(n_oct - 1, 1, w)
    last = pl.multiple_of((n_oct - 1) * OCTET, OCTET)
    total = out_ref[pl.ds(last, OCTET), :]
    for tt in range(GATHER_GROUP):
        total = combine_token(n_oct - 1, 1, tt, 1, w, total)
    out_ref[pl.ds(last, OCTET), :] = total

    @pl.when(i == n_tiles - 1)
    def _():
        for gg in range(GROUPS_PER_OCTET):
            for tt in range(GATHER_GROUP):
                wait(gg, tt, 0)


def _experts(idx, xn, gates, h1, table, *, tm):
    t, d = xn.shape
    n_chunk = d // LANES
    row = lambda w: pl.BlockSpec((tm, w), lambda i: (i, 0))
    slot_ids = jnp.arange(PEER_SLOTS)
    expand = jnp.zeros((PEER_SLOTS, 2 * PEER_SLOTS), BF16).at[slot_ids, 2 * slot_ids + 1].set(1)
    idx_pad = jnp.pad(idx, ((0, OCTET), (0, 0)))
    gather_buf = pltpu.VMEM((GATHER_GROUP, n_chunk, 2 * PEER_SLOTS, LANES), BF16)
    return pl.pallas_call(
        functools.partial(_experts_kernel, tm=tm, n_tiles=t // tm),
        grid=(t // tm,),
        in_specs=[pl.BlockSpec(memory_space=pl.ANY), row(d), row(PEER_SLOTS), row(d), _const_spec(expand.shape),
                  pl.BlockSpec(memory_space=pl.ANY)],
        out_specs=row(d),
        out_shape=jax.ShapeDtypeStruct((t, d), F32),
        scratch_shapes=[pltpu.SMEM((tm + OCTET, PEER_SLOTS), jnp.int32),
                        gather_buf, gather_buf, gather_buf, gather_buf,
                        pltpu.VMEM((2 * PEER_SLOTS, LANES), F32),
                        pltpu.SemaphoreType.DMA((2, GROUPS_PER_OCTET, GATHER_GROUP)),
                        pltpu.SemaphoreType.DMA((1,))],
        compiler_params=_params("arbitrary"),
        name="peer_experts",
    )(idx_pad, xn, gates, h1, expand, table)


def _pair_expert_table(expert_u, expert_v):
    e, d = expert_u.shape
    u = expert_u.reshape(e, d // LANES, 1, LANES)
    v = expert_v.reshape(e, d // LANES, 1, LANES)
    return jnp.concatenate([u, v], axis=2).astype(BF16)


def _ple_kernel(h_ref, p_ref, g_ref, wg_ref, wp_ref, o_ref):
    h = h_ref[...]
    xn = _rms(h) * g_ref[...]
    gate = jax.nn.sigmoid(_dot(xn.astype(BF16), wg_ref[...]))
    o_ref[...] = h + gate * _dot(p_ref[...].astype(BF16), wp_ref[...])


def _ple(h2, p2, g, wg, wp, *, tm):
    t, d = h2.shape
    row = lambda w: pl.BlockSpec((tm, w), lambda i: (i, 0))
    return pl.pallas_call(
        _ple_kernel,
        grid=(t // tm,),
        in_specs=[row(d), row(p2.shape[1]), _const_spec(g.shape), _const_spec(wg.shape), _const_spec(wp.shape)],
        out_specs=row(d),
        out_shape=jax.ShapeDtypeStruct((t, d), F32),
        compiler_params=_params("parallel"),
        name="ple",
    )(h2, p2, g, wg, wp)


def _layer(h2, p2, pos2, batch, seq, mix_norm_gain, w_in, w_pool, pool_scale, q_lat_gain, kv_lat_gain, w_uq, w_ukv,
           q_norm_gain, k_norm_gain, w_out, ffn_norm_gain, w_pq, sub_k1, sub_k2, expert_u, expert_v, ple_norm_gain,
           w_ple_gate, w_ple_proj):
    t, d = h2.shape
    q_lora = q_lat_gain.shape[0]
    kv_lora = kv_lat_gain.shape[0]
    n_heads = w_uq.shape[1] // QK_HEAD
    pool_w = w_in.shape[1] - q_lora - kv_lora - QK_ROPE
    rowv = lambda a: a.reshape(1, -1).astype(F32)

    win = jnp.pad(w_in, ((0, 0), (0, LANES - QK_ROPE))).astype(BF16)
    wuq = jnp.pad(w_uq.reshape(q_lora, n_heads, QK_HEAD), ((0, 0), (0, 0), (0, HEAD_SLOT - QK_HEAD)))
    wuq = wuq.reshape(q_lora, n_heads * HEAD_SLOT).astype(BF16)
    wukv = w_ukv.reshape(kv_lora, n_heads, 2, QK_NOPE).transpose(0, 2, 1, 3).reshape(kv_lora, -1).astype(BF16)
    pad_gain = lambda g: jnp.pad(g, (0, HEAD_SLOT - QK_HEAD)).reshape(1, HEAD_SLOT).astype(F32)
    gqh = pad_gain(q_norm_gain) * (QK_HEAD ** -0.5)
    gkh = pad_gain(k_norm_gain)
    inv_freq = ROPE_THETA ** (-jnp.arange(0, QK_ROPE, 2, dtype=F32) / QK_ROPE)
    invf = jnp.concatenate([inv_freq, inv_freq, jnp.zeros((LANES - QK_ROPE,), F32)]).reshape(1, LANES)

    zp, q, k, v = _mix_in(h2, pos2, rowv(mix_norm_gain), win, rowv(q_lat_gain), rowv(kv_lat_gain), wuq, wukv, gqh,
                          gkh, invf, pool_w=pool_w, q_lora=q_lora, kv_lora=kv_lora, n_heads=n_heads, tm=256)
    shp = lambda a: a.reshape(batch, seq, a.shape[1])
    o = _attention(shp(q), shp(k), shp(v), n_heads=n_heads, tq=1024, tk=512)
    h1 = _mix_out(zp, o.reshape(t, -1), h2, w_pool.astype(BF16), rowv(pool_scale), w_out.astype(BF16), seq=seq,
                  tm=256)
    xn, idx, gates = _route(h1, rowv(ffn_norm_gain), w_pq.astype(BF16), sub_k1.astype(BF16), sub_k2.astype(BF16),
                            tm=256)
    h3 = _experts(idx, xn, gates, h1, _pair_expert_table(expert_u, expert_v), tm=256)
    return _ple(h3, p2, rowv(ple_norm_gain), w_ple_gate.astype(BF16), w_ple_proj.astype(BF16), tm=256)


def kernel(x, p, positions, mix_norm_gain, w_in, w_pool, pool_scale, q_lat_gain, kv_lat_gain, w_uq, w_ukv,
           q_norm_gain, k_norm_gain, w_out, ffn_norm_gain, w_pq, sub_k1, sub_k2, expert_u, expert_v, ple_norm_gain,
           w_ple_gate, w_ple_proj):
    batch, seq, d = x.shape
    t = batch * seq
    h = x.reshape(t, d)
    pos2 = positions.reshape(t, 1).astype(jnp.int32)
    for i in range(w_in.shape[0]):
        h = _layer(h, p[i].reshape(t, -1), pos2, batch, seq, mix_norm_gain[i], w_in[i], w_pool[i], pool_scale[i],
                   q_lat_gain[i], kv_lat_gain[i], w_uq[i], w_ukv[i], q_norm_gain[i], k_norm_gain[i], w_out[i],
                   ffn_norm_gain[i], w_pq[i], sub_k1[i], sub_k2[i], expert_u[i], expert_v[i], ple_norm_gain[i],
                   w_ple_gate[i], w_ple_proj[i])
    return h.reshape(batch, seq, d)
```

```python
import functools

import jax
import jax.numpy as jnp
from jax import lax
from jax.experimental import pallas as pl
from jax.experimental.pallas import tpu as pltpu

F32 = jnp.float32
BF16 = jnp.bfloat16

EPS = 1e-6
CHUNK = 64
POOL_WINDOWS = (2, 4, 8, 16)
POOL_HALO = 16
V_HEAD = 128
QK_NOPE = 128
QK_ROPE = 64
QK_HEAD = QK_NOPE + QK_ROPE
HEAD_SLOT = 256
ROPE_THETA = 10000.0
PEER_HEADS = 8
PEER_NKEYS = 128
PEER_HALF = 128
PEER_TOPK = 16
PEER_SLOTS = PEER_HEADS * PEER_TOPK

LANES = 128
VMEM_LIMIT_BYTES = 56 * 1024 * 1024

OCTET = 8
GATHER_GROUP = 4
GROUPS_PER_OCTET = OCTET // GATHER_GROUP


def _rms(x):
    return x * lax.rsqrt(jnp.mean(x * x, axis=-1, keepdims=True) + EPS)


def _dot(a, b):
    return jnp.dot(a, b, preferred_element_type=F32)


def _dot_nt(a, b):
    return lax.dot_general(a, b, (((1,), (1,)), ((), ())), preferred_element_type=F32)


def _params(*sem):
    return pltpu.CompilerParams(dimension_semantics=sem, vmem_limit_bytes=VMEM_LIMIT_BYTES)


def _const_spec(shape):
    nd = len(shape)
    return pl.BlockSpec(shape, lambda *_: (0,) * nd)


def _mix_in_kernel(x_ref, pos_ref, g_ref, win_ref, gq_ref, gkv_ref, wuq_ref, wukv_ref, gqh_ref, gkh_ref,
                   invf_ref, zp_ref, q_ref, k_ref, v_ref, *, pool_w, q_lora, kv_lora, n_heads):
    xn = _rms(x_ref[...]) * g_ref[...]
    z = _dot(xn.astype(BF16), win_ref[...])
    zp_ref[...] = z[:, :pool_w]
    c1 = pool_w + q_lora
    c2 = c1 + kv_lora
    cq = _rms(z[:, pool_w:c1]) * gq_ref[...]
    ckv = _rms(z[:, c1:c2]) * gkv_ref[...]
    kr = z[:, c2:c2 + LANES]
    qx = _dot(cq.astype(BF16), wuq_ref[...])
    kvx = _dot(ckv.astype(BF16), wukv_ref[...])

    ang = pos_ref[...].astype(F32) * invf_ref[...]
    cos = jnp.cos(ang)
    sin = jnp.sin(ang)
    lane = lax.broadcasted_iota(jnp.int32, ang.shape, 1)
    first_half = lane < QK_ROPE // 2
    sin_signed = jnp.where(first_half, -sin, sin)

    def rope(y):
        partner = jnp.where(first_half, pltpu.roll(y, LANES - QK_ROPE // 2, 1), pltpu.roll(y, QK_ROPE // 2, 1))
        return y * cos + partner * sin_signed

    gq = gqh_ref[...]
    gk = gkh_ref[...]
    inv_w = 1.0 / QK_HEAD
    kr_ss = jnp.sum(kr * kr, axis=-1, keepdims=True)
    kr_roped = rope(kr * gk[:, QK_NOPE:])
    for h in range(n_heads):
        lo = h * HEAD_SLOT
        slab = qx[:, lo:lo + HEAD_SLOT]
        r = lax.rsqrt(jnp.sum(slab * slab, axis=-1, keepdims=True) * inv_w + EPS)
        q_ref[:, lo:lo + QK_NOPE] = (slab[:, :QK_NOPE] * r * gq[:, :QK_NOPE]).astype(BF16)
        q_ref[:, lo + QK_NOPE:lo + HEAD_SLOT] = rope(slab[:, QK_NOPE:] * r * gq[:, QK_NOPE:]).astype(BF16)
        kn = kvx[:, h * QK_NOPE:(h + 1) * QK_NOPE]
        rk = lax.rsqrt((jnp.sum(kn * kn, axis=-1, keepdims=True) + kr_ss) * inv_w + EPS)
        k_ref[:, lo:lo + QK_NOPE] = (kn * rk * gk[:, :QK_NOPE]).astype(BF16)
        k_ref[:, lo + QK_NOPE:lo + HEAD_SLOT] = (kr_roped * rk).astype(BF16)
    v_ref[...] = kvx[:, n_heads * QK_NOPE:].astype(BF16)


def _mix_in(x2, pos2, g, win, gq, gkv, wuq, wukv, gqh, gkh, invf, *, pool_w, q_lora, kv_lora, n_heads, tm):
    t, d = x2.shape
    kern = functools.partial(_mix_in_kernel, pool_w=pool_w, q_lora=q_lora, kv_lora=kv_lora, n_heads=n_heads)
    row = lambda w: pl.BlockSpec((tm, w), lambda i: (i, 0))
    return pl.pallas_call(
        kern,
        grid=(t // tm,),
        in_specs=[row(d), row(1), _const_spec(g.shape), _const_spec(win.shape), _const_spec(gq.shape),
                  _const_spec(gkv.shape), _const_spec(wuq.shape), _const_spec(wukv.shape), _const_spec(gqh.shape),
                  _const_spec(gkh.shape), _const_spec(invf.shape)],
        out_specs=[row(pool_w), row(n_heads * HEAD_SLOT), row(n_heads * HEAD_SLOT), row(n_heads * V_HEAD)],
        out_shape=[jax.ShapeDtypeStruct((t, pool_w), F32),
                   jax.ShapeDtypeStruct((t, n_heads * HEAD_SLOT), BF16),
                   jax.ShapeDtypeStruct((t, n_heads * HEAD_SLOT), BF16),
                   jax.ShapeDtypeStruct((t, n_heads * V_HEAD), BF16)],
        compiler_params=_params("parallel"),
        name="mix_in",
    )(x2, pos2, g, win, gq, gkv, wuq, wukv, gqh, gkh, invf)


def _attn_kernel(q_ref, k_ref, v_ref, o_ref, *, tq, tk):
    qi = pl.program_id(2)
    n_sub = tq // tk
    qs = [q_ref[s * tk:(s + 1) * tk, :] for s in range(n_sub)]
    q_chunks = [(qi * tq + s * tk + lax.broadcasted_iota(jnp.int32, (tk, 1), 0)) // CHUNK for s in range(n_sub)]

    def body(j, carry):
        start = pl.multiple_of(j * tk, tk)
        kj = k_ref[pl.ds(start, tk), :]
        vj = v_ref[pl.ds(start, tk), :]
        k_chunk = (start + lax.broadcasted_iota(jnp.int32, (1, tk), 1)) // CHUNK
        out = []
        for s in range(n_sub):
            m, l, acc = carry[s]
            sc = jnp.where(k_chunk <= q_chunks[s], _dot_nt(qs[s], kj), -jnp.inf)
            m_new = jnp.maximum(m, jnp.max(sc, axis=-1, keepdims=True))
            alpha = jnp.exp(m - m_new)
            p = jnp.exp(sc - m_new)
            l = alpha * l + jnp.sum(p, axis=-1, keepdims=True)
            acc = alpha * acc + _dot(p.astype(BF16), vj)
            out.append((m_new, l, acc))
        return tuple(out)

    init = tuple((jnp.full((tk, 1), -jnp.inf, F32), jnp.zeros((tk, 1), F32), jnp.zeros((tk, V_HEAD), F32))
                 for _ in range(n_sub))
    res = lax.fori_loop(0, (qi + 1) * n_sub, body, init)
    for s in range(n_sub):
        _, l, acc = res[s]
        o_ref[s * tk:(s + 1) * tk, :] = (acc / l).astype(BF16)


def _attention(q3, k3, v3, *, n_heads, tq, tk):
    b, s, _ = q3.shape
    return pl.pallas_call(
        functools.partial(_attn_kernel, tq=tq, tk=tk),
        grid=(b, n_heads, s // tq),
        in_specs=[pl.BlockSpec((None, tq, HEAD_SLOT), lambda bi, h, i: (bi, i, h)),
                  pl.BlockSpec((None, s, HEAD_SLOT), lambda bi, h, i: (bi, 0, h)),
                  pl.BlockSpec((None, s, V_HEAD), lambda bi, h, i: (bi, 0, h))],
        out_specs=pl.BlockSpec((None, tq, V_HEAD), lambda bi, h, i: (bi, i, h)),
        out_shape=jax.ShapeDtypeStruct((b, s, n_heads * V_HEAD), BF16),
        compiler_params=_params("parallel", "parallel", "arbitrary"),
        name="attn",
    )(q3, k3, v3)


def _mix_out_kernel(zp_ref, halo_ref, o_ref, x_ref, wpool_ref, pscale_ref, wout_ref, h_ref, ext_ref, *, seq, tm):
    t0 = (pl.program_id(0) * tm) % seq
    pool_w = zp_ref.shape[1]
    gw = pool_w // len(POOL_WINDOWS)
    ext_ref[0:POOL_HALO, :] = jnp.where(t0 == 0, 0.0, halo_ref[...])
    ext_ref[POOL_HALO:, :] = zp_ref[...]
    tpos = t0 + lax.broadcasted_iota(jnp.int32, (tm, 1), 0)
    acc = x_ref[...] + _dot(o_ref[...], wout_ref[pool_w:, :])
    for g, w in enumerate(POOL_WINDOWS):
        lo = g * gw
        cur = ext_ref[POOL_HALO:POOL_HALO + tm, lo:lo + gw]
        win_sum = cur
        for j in range(1, w):
            win_sum = win_sum + ext_ref[POOL_HALO - j:POOL_HALO - j + tm, lo:lo + gw]
        cnt = jnp.minimum(tpos + 1, w).astype(F32)
        dlt = win_sum / cnt - cur
        y = _dot(dlt.astype(BF16), wpool_ref[g]) * pscale_ref[:, lo:lo + gw]
        acc = acc + _dot(y.astype(BF16), wout_ref[lo:lo + gw, :])
    h_ref[...] = acc


def _mix_out(zp, o2, x2, wpool, pscale, wout, *, seq, tm):
    t, d = x2.shape
    pool_w = zp.shape[1]
    hb = tm // POOL_HALO
    row = lambda w: pl.BlockSpec((tm, w), lambda i: (i, 0))
    return pl.pallas_call(
        functools.partial(_mix_out_kernel, seq=seq, tm=tm),
        grid=(t // tm,),
        in_specs=[row(pool_w),
                  pl.BlockSpec((POOL_HALO, pool_w), lambda i: (jnp.maximum(i * hb - 1, 0), 0)),
                  row(o2.shape[1]), row(d), _const_spec(wpool.shape), _const_spec(pscale.shape),
                  _const_spec(wout.shape)],
        out_specs=row(d),
        out_shape=jax.ShapeDtypeStruct((t, d), F32),
        scratch_shapes=[pltpu.VMEM((tm + POOL_HALO, pool_w), F32)],
        compiler_params=_params("parallel"),
        name="mix_out",
    )(zp, zp, o2, x2, wpool, pscale, wout)


def _topk_cols(s, n, rows=None):
    tm = s.shape[1]
    if rows is None:
        rows = lax.broadcasted_iota(jnp.int32, s.shape, 0).astype(F32)
    slot = lax.broadcasted_iota(jnp.int32, (PEER_TOPK, tm), 0)
    vals = jnp.zeros((PEER_TOPK, tm), F32)
    idxs = jnp.zeros((PEER_TOPK, tm), F32)
    for k in range(PEER_TOPK):
        m = jnp.max(s, axis=0, keepdims=True)
        i = jnp.min(jnp.where(s == m, rows, float(n)), axis=0, keepdims=True)
        vals = jnp.where(slot == k, m, vals)
        idxs = jnp.where(slot == k, i, idxs)
        s = jnp.where(rows == i, -jnp.inf, s)
    return vals, idxs


def _route_kernel(h_ref, g_ref, wpq_ref, k1_ref, k2_ref, xn_ref, idx_ref, gate_ref, qp_ref, idxt_ref, gatet_ref):
    xn = _rms(h_ref[...]) * g_ref[...]
    xn_ref[...] = xn
    qp_ref[...] = _dot(xn.astype(BF16), wpq_ref[...]).astype(BF16)

    def head(hd, carry):
        off = pl.multiple_of(hd * 2 * PEER_HALF, 2 * PEER_HALF)
        s1 = _dot_nt(k1_ref[hd], qp_ref[:, pl.ds(off, PEER_HALF)])
        s2 = _dot_nt(k2_ref[hd], qp_ref[:, pl.ds(off + PEER_HALF, PEER_HALF)])
        v1, i1 = _topk_cols(s1, PEER_NKEYS)
        v2, i2 = _topk_cols(s2, PEER_NKEYS)
        half_k = PEER_TOPK // 2
        b_ids = lax.broadcasted_iota(jnp.int32, (half_k, 1), 0)
        pieces = [v1[0:1, :] + v2]
        ids = [lax.broadcasted_iota(jnp.int32, (PEER_TOPK, 1), 0)]
        for a in range(1, PEER_TOPK):
            pieces.append(jnp.where(b_ids < PEER_TOPK // (a + 1), v1[a:a + 1, :] + v2[:half_k, :], -jnp.inf))
            ids.append(a * PEER_TOPK + b_ids)
        cand = jnp.concatenate(pieces, axis=0)
        vs, ci = _topk_cols(cand, PEER_TOPK * PEER_TOPK, jnp.concatenate(ids, axis=0).astype(F32))
        ci = ci.astype(jnp.int32)
        a_sel = ci // PEER_TOPK
        b_sel = ci % PEER_TOPK
        e1 = jnp.zeros_like(vs)
        e2 = jnp.zeros_like(vs)
        for a in range(PEER_TOPK):
            e1 = jnp.where(a_sel == a, i1[a:a + 1, :], e1)
            e2 = jnp.where(b_sel == a, i2[a:a + 1, :], e2)
        ex = jnp.exp(vs - jnp.max(vs, axis=0, keepdims=True))
        r0 = pl.multiple_of(hd * PEER_TOPK, PEER_TOPK)
        idxt_ref[pl.ds(r0, PEER_TOPK), :] = (e1 * PEER_NKEYS + e2).astype(jnp.int32)
        gatet_ref[pl.ds(r0, PEER_TOPK), :] = ex / jnp.sum(ex, axis=0, keepdims=True)
        return carry

    lax.fori_loop(0, PEER_HEADS, head, 0)
    idx_ref[...] = idxt_ref[...].T
    gate_ref[...] = gatet_ref[...].T


def _route(h1, g, wpq, k1, k2, *, tm):
    t, d = h1.shape
    row = lambda w: pl.BlockSpec((tm, w), lambda i: (i, 0))
    return pl.pallas_call(
        _route_kernel,
        grid=(t // tm,),
        in_specs=[row(d), _const_spec(g.shape), _const_spec(wpq.shape), _const_spec(k1.shape),
                  _const_spec(k2.shape)],
        out_specs=[row(d), row(PEER_SLOTS), row(PEER_SLOTS)],
        out_shape=[jax.ShapeDtypeStruct((t, d), F32),
                   jax.ShapeDtypeStruct((t, PEER_SLOTS), jnp.int32),
                   jax.ShapeDtypeStruct((t, PEER_SLOTS), F32)],
        scratch_shapes=[pltpu.VMEM((tm, wpq.shape[1]), BF16),
                        pltpu.VMEM((PEER_SLOTS, tm), jnp.int32),
                        pltpu.VMEM((PEER_SLOTS, tm), F32)],
        compiler_params=_params("parallel"),
        name="peer_route",
    )(h1, g, wpq, k1, k2)


def _experts_kernel(idx_hbm, xn_ref, gate_ref, h_ref, expand_ref, tab_hbm, p_ref, pg_ref, wg_ref, wp_ref, out_ref,
                    idx_smem, b00, b01, b10, b11, cols_ref, sem, idx_sem, *, tm, n_tiles):
    bufs = ((b00, b01), (b10, b11))
    n_chunk = b00.shape[1]
    n_oct = tm // OCTET
    i = pl.program_id(0)
    idx_cp = pltpu.make_async_copy(idx_hbm.at[pl.ds(i * tm, tm + OCTET), :], idx_smem, idx_sem.at[0])
    idx_cp.start()
    idx_cp.wait()

    def issue(o, gg, tt, half):
        t = o * OCTET + gg * GATHER_GROUP + tt
        for k in range(PEER_SLOTS):
            pltpu.make_async_copy(tab_hbm.at[idx_smem[t, k]], bufs[half][gg].at[tt, :, pl.ds(2 * k, 2), :],
                                  sem.at[half, gg, tt]).start(priority=k % 2)

    def wait(gg, tt, half):
        pltpu.make_async_copy(bufs[half][gg].at[tt], bufs[half][gg].at[tt], sem.at[half, gg, tt]).wait()

    lane = lax.broadcasted_iota(jnp.int32, (2 * PEER_SLOTS, LANES), 1)
    sub = lax.broadcasted_iota(jnp.int32, (OCTET, 2 * PEER_SLOTS), 0)

    def combine_token(o, gg, tt, half, wgt, total):
        j = gg * GATHER_GROUP + tt
        wj = jnp.where(sub == j, wgt, 0.0).astype(BF16)
        return total + jnp.concatenate([_dot(wj, bufs[half][gg][tt, c]) for c in range(n_chunk)], axis=1)

    def group(o, gg, half, prev):
        buf = bufs[half][gg]
        base = pl.multiple_of(o * OCTET, OCTET)
        x8 = xn_ref[pl.ds(base, OCTET), :]
        cols = jnp.zeros((2 * PEER_SLOTS, LANES), F32)
        if prev is not None:
            po, pg, ph, pw = prev
            pbase = pl.multiple_of(po * OCTET, OCTET)
            total = (h_ref if pg == 0 else out_ref)[pl.ds(pbase, OCTET), :]
        for tt in range(GATHER_GROUP):
            wait(gg, tt, half)
            issue(o + 1, gg, tt, 1 - half)
            j = gg * GATHER_GROUP + tt
            part = jnp.zeros((2 * PEER_SLOTS, LANES), F32)
            for c in range(n_chunk):
                part = part + buf[tt, c].astype(F32) * x8[j:j + 1, c * LANES:(c + 1) * LANES]
            cols = jnp.where(lane == j, jnp.sum(part, axis=-1, keepdims=True), cols)
            if prev is not None:
                total = combine_token(po, pg, tt, ph, pw, total)
        if prev is not None:
            out_ref[pl.ds(pbase, OCTET), :] = total
        cols_ref[...] = cols
        act = cols_ref[pl.ds(0, PEER_SLOTS, stride=2), :].T[:OCTET, :]
        wgt = jax.nn.gelu(act) * gate_ref[pl.ds(base, OCTET), :]
        return _dot(wgt.astype(BF16), expand_ref[...])

    def octet(o, half, w_prev):
        w0 = group(o, 0, half, None if w_prev is None else (o - 1, 1, 1 - half, w_prev))
        return group(o, 1, half, (o, 0, half, w0))

    @pl.when(i == 0)
    def _():
        for gg in range(GROUPS_PER_OCTET):
            for tt in range(GATHER_GROUP):
                issue(0, gg, tt, 0)

    def octet_pair(pi, w):
        w = octet(2 * pi + 1, 1, w)
        return octet(2 * pi + 2, 0, w)

    w = octet(0, 0, None)
    w = lax.fori_loop(0, (n_oct - 2) // 2, octet_pair, w)
    w = octet(n_oct - 1, 1, w)
    last = pl.multiple_of((n_oct - 1) * OCTET, OCTET)
    total = out_ref[pl.ds(last, OCTET), :]
    for tt in range(GATHER_GROUP):
        total = combine_token(n_oct - 1, 1, tt, 1, w, total)
    out_ref[pl.ds(last, OCTET), :] = total

    h = out_ref[...]
    ple_gate = jax.nn.sigmoid(_dot((_rms(h) * pg_ref[...]).astype(BF16), wg_ref[...]))
    out_ref[...] = h + ple_gate * _dot(p_ref[...].astype(BF16), wp_ref[...])

    @pl.when(i == n_tiles - 1)
    def _():
        for gg in range(GROUPS_PER_OCTET):
            for tt in range(GATHER_GROUP):
                wait(gg, tt, 0)


def _experts(idx, xn, gates, h1, table, p2, pg, wg, wp, *, tm):
    t, d = xn.shape
    n_chunk = d // LANES
    row = lambda w: pl.BlockSpec((tm, w), lambda i: (i, 0))
    slot_ids = jnp.arange(PEER_SLOTS)
    expand = jnp.zeros((PEER_SLOTS, 2 * PEER_SLOTS), BF16).at[slot_ids, 2 * slot_ids + 1].set(1)
    idx_pad = jnp.pad(idx, ((0, OCTET), (0, 0)))
    gather_buf = pltpu.VMEM((GATHER_GROUP, n_chunk, 2 * PEER_SLOTS, LANES), BF16)
    return pl.pallas_call(
        functools.partial(_experts_kernel, tm=tm, n_tiles=t // tm),
        grid=(t // tm,),
        in_specs=[pl.BlockSpec(memory_space=pl.ANY), row(d), row(PEER_SLOTS), row(d), _const_spec(expand.shape),
                  pl.BlockSpec(memory_space=pl.ANY), row(p2.shape[1]), _const_spec(pg.shape), _const_spec(wg.shape),
                  _const_spec(wp.shape)],
        out_specs=row(d),
        out_shape=jax.ShapeDtypeStruct((t, d), F32),
        scratch_shapes=[pltpu.SMEM((tm + OCTET, PEER_SLOTS), jnp.int32),
                        gather_buf, gather_buf, gather_buf, gather_buf,
                        pltpu.VMEM((2 * PEER_SLOTS, LANES), F32),
                        pltpu.SemaphoreType.DMA((2, GROUPS_PER_OCTET, GATHER_GROUP)),
                        pltpu.SemaphoreType.DMA((1,))],
        compiler_params=_params("arbitrary"),
        name="peer_experts",
    )(idx_pad, xn, gates, h1, expand, table, p2, pg, wg, wp)


def _pair_expert_table(expert_u, expert_v):
    e, d = expert_u.shape
    u = expert_u.reshape(e, d // LANES, 1, LANES)
    v = expert_v.reshape(e, d // LANES, 1, LANES)
    return jnp.concatenate([u, v], axis=2).astype(BF16)


def _layer(h2, p2, pos2, batch, seq, mix_norm_gain, w_in, w_pool, pool_scale, q_lat_gain, kv_lat_gain, w_uq, w_ukv,
           q_norm_gain, k_norm_gain, w_out, ffn_norm_gain, w_pq, sub_k1, sub_k2, expert_u, expert_v, ple_norm_gain,
           w_ple_gate, w_ple_proj):
    t, d = h2.shape
    q_lora = q_lat_gain.shape[0]
    kv_lora = kv_lat_gain.shape[0]
    n_heads = w_uq.shape[1] // QK_HEAD
    pool_w = w_in.shape[1] - q_lora - kv_lora - QK_ROPE
    rowv = lambda a: a.reshape(1, -1).astype(F32)

    win = jnp.pad(w_in, ((0, 0), (0, LANES - QK_ROPE))).astype(BF16)
    wuq = jnp.pad(w_uq.reshape(q_lora, n_heads, QK_HEAD), ((0, 0), (0, 0), (0, HEAD_SLOT - QK_HEAD)))
    wuq = wuq.reshape(q_lora, n_heads * HEAD_SLOT).astype(BF16)
    wukv = w_ukv.reshape(kv_lora, n_heads, 2, QK_NOPE).transpose(0, 2, 1, 3).reshape(kv_lora, -1).astype(BF16)
    pad_gain = lambda g: jnp.pad(g, (0, HEAD_SLOT - QK_HEAD)).reshape(1, HEAD_SLOT).astype(F32)
    gqh = pad_gain(q_norm_gain) * (QK_HEAD ** -0.5)
    gkh = pad_gain(k_norm_gain)
    inv_freq = ROPE_THETA ** (-jnp.arange(0, QK_ROPE, 2, dtype=F32) / QK_ROPE)
    invf = jnp.concatenate([inv_freq, inv_freq, jnp.zeros((LANES - QK_ROPE,), F32)]).reshape(1, LANES)

    zp, q, k, v = _mix_in(h2, pos2, rowv(mix_norm_gain), win, rowv(q_lat_gain), rowv(kv_lat_gain), wuq, wukv, gqh,
                          gkh, invf, pool_w=pool_w, q_lora=q_lora, kv_lora=kv_lora, n_heads=n_heads, tm=256)
    shp = lambda a: a.reshape(batch, seq, a.shape[1])
    o = _attention(shp(q), shp(k), shp(v), n_heads=n_heads, tq=1024, tk=512)
    h1 = _mix_out(zp, o.reshape(t, -1), h2, w_pool.astype(BF16), rowv(pool_scale), w_out.astype(BF16), seq=seq,
                  tm=256)
    xn, idx, gates = _route(h1, rowv(ffn_norm_gain), w_pq.astype(BF16), sub_k1.astype(BF16), sub_k2.astype(BF16),
                            tm=256)
    return _experts(idx, xn, gates, h1, _pair_expert_table(expert_u, expert_v), p2, rowv(ple_norm_gain),
                    w_ple_gate.astype(BF16), w_ple_proj.astype(BF16), tm=256)


def kernel(x, p, positions, mix_norm_gain, w_in, w_pool, pool_scale, q_lat_gain, kv_lat_gain, w_uq, w_ukv,
           q_norm_gain, k_norm_gain, w_out, ffn_norm_gain, w_pq, sub_k1, sub_k2, expert_u, expert_v, ple_norm_gain,
           w_ple_gate, w_ple_proj):
    batch, seq, d = x.shape
    t = batch * seq
    h = x.reshape(t, d)
    pos2 = positions.reshape(t, 1).astype(jnp.int32)
    for i in range(w_in.shape[0]):
        h = _layer(h, p[i].reshape(t, -1), pos2, batch, seq, mix_norm_gain[i], w_in[i], w_pool[i], pool_scale[i],
                   q_lat_gain[i], kv_lat_gain[i], w_uq[i], w_ukv[i], q_norm_gain[i], k_norm_gain[i], w_out[i],
                   ffn_norm_gain[i], w_pq[i], sub_k1[i], sub_k2[i], expert_u[i], expert_v[i], ple_norm_gain[i],
                   w_ple_gate[i], w_ple_proj[i])
    return h.reshape(batch, seq, d)
```
